```python
import jax, jax.numpy as jnp
from jax import lax
import numpy as np

D_MODEL = 2048
BATCH = 1
SEQ = 16384
DEPTH = 2
DEC_BATCH = 16
DEC_SEQ = 2048
PAST_LEN = 128

GRID_W = 64
PLE_DIM = 256
EPS = 1e-6
Q_BLOCK = 128
ROPE_THETA = 10000.0

NA_HEADS = 8
NA_HEAD_DIM = 64
NA_WIN_H = 8
NA_WIN_W = 16
NA_WIDTH = NA_HEADS * NA_HEAD_DIM

MLA_HEADS = 6
MLA_NOPE = 128
MLA_ROPE = 64
MLA_V = 128
MLA_Q_LORA = 512
MLA_KV_LORA = 256
MLA_WIDTH = MLA_HEADS * MLA_V

GQA_HEADS = 12
GQA_KV_HEADS = 4
GQA_HEAD_DIM = 64
GQA_GROUP = GQA_HEADS // GQA_KV_HEADS
GQA_WIDTH = GQA_HEADS * GQA_HEAD_DIM
GQA_KV_WIDTH = GQA_KV_HEADS * GQA_HEAD_DIM

MIX_WIDTH = NA_WIDTH + MLA_WIDTH + GQA_WIDTH
IN_SPLITS = (NA_WIDTH, NA_WIDTH, NA_WIDTH, NA_WIDTH,
             MLA_Q_LORA, MLA_KV_LORA, MLA_ROPE, MLA_WIDTH,
             GQA_WIDTH, GQA_KV_WIDTH, GQA_KV_WIDTH, GQA_WIDTH)
IN_COLS = sum(IN_SPLITS)

kernel_name = "hybrid_na_mla_gqa_encoder"


def rms_norm(x, g):
    xf = x.astype(jnp.float32)
    y = xf * lax.rsqrt(jnp.mean(xf * xf, axis=-1, keepdims=True) + EPS)
    return (y * g.astype(jnp.float32)).astype(x.dtype)


def rope_tables(pos, dim, dtype):
    inv = ROPE_THETA ** (-jnp.arange(0, dim, 2, dtype=jnp.float32) / dim)
    ang = pos.astype(jnp.float32)[:, None] * inv[None, :]
    ang = jnp.concatenate([ang, ang], axis=-1)
    return jnp.cos(ang).astype(dtype), jnp.sin(ang).astype(dtype)


def apply_rope(x, cos, sin):
    half = x.shape[-1] // 2
    x1, x2 = x[..., :half], x[..., half:]
    rot = jnp.concatenate([-x2, x1], axis=-1)
    return x * cos + rot * sin


def neighborhood_attention(q, k, v, rpb):
    B, S, H, D = q.shape
    rows = S // GRID_W
    kh = min(NA_WIN_H, rows)
    kw = min(NA_WIN_W, GRID_W)
    r = jnp.arange(rows)
    c = jnp.arange(GRID_W)
    rs = jnp.clip(r - kh // 2, 0, rows - kh)
    cs = jnp.clip(c - kw // 2, 0, GRID_W - kw)
    key_r = rs[:, None] + jnp.arange(kh)
    key_c = cs[:, None] + jnp.arange(kw)
    idx = (key_r[:, None, :, None] * GRID_W + key_c[None, :, None, :]).reshape(rows, GRID_W, kh * kw)
    di = key_r - r[:, None] + (NA_WIN_H - 1)
    dj = key_c - c[:, None] + (NA_WIN_W - 1)
    qb = q.reshape(B, rows, GRID_W, H, D).transpose(1, 0, 2, 3, 4)
    scale = D ** -0.5

    def row_block(args):
        q_b, idx_b, di_b = args
        k_g = jnp.take(k, idx_b, axis=1)
        v_g = jnp.take(v, idx_b, axis=1)
        s = jnp.einsum('bqhd,bqkhd->bhqk', q_b, k_g).astype(jnp.float32) * scale
        bias = rpb[:, di_b[:, None, None], dj[None, :, :]]
        bias = bias.transpose(0, 2, 1, 3).reshape(H, GRID_W, kh * kw).astype(jnp.float32)
        p = jax.nn.softmax(s + bias[None], axis=-1).astype(v.dtype)
        return jnp.einsum('bhqk,bqkhd->bqhd', p, v_g)

    out = lax.map(row_block, (qb, idx, di))
    return out.transpose(1, 0, 2, 3, 4).reshape(B, S, H * D)


def mla_attention(q_nope, q_rope, k_nope, k_rope, v):
    B, S, H, _ = q_nope.shape
    nb = S // Q_BLOCK
    qn = q_nope.reshape(B, nb, Q_BLOCK, H, MLA_NOPE).transpose(1, 0, 2, 3, 4)
    qr = q_rope.reshape(B, nb, Q_BLOCK, H, MLA_ROPE).transpose(1, 0, 2, 3, 4)
    scale = (MLA_NOPE + MLA_ROPE) ** -0.5

    def q_block(args):
        qn_b, qr_b = args
        s = (jnp.einsum('bqhd,bkhd->bhqk', qn_b, k_nope)
             + jnp.einsum('bqhr,bkr->bhqk', qr_b, k_rope))
        p = jax.nn.softmax(s.astype(jnp.float32) * scale, axis=-1).astype(v.dtype)
        return jnp.einsum('bhqk,bkhd->bqhd', p, v)

    out = lax.map(q_block, (qn, qr))
    return out.transpose(1, 0, 2, 3, 4).reshape(B, S, H * MLA_V)


def gqa_attention(q, k, v):
    B, S, _, D = q.shape
    nb = S // Q_BLOCK
    qg = q.reshape(B, nb, Q_BLOCK, GQA_KV_HEADS, GQA_GROUP, D).transpose(1, 0, 2, 3, 4, 5)
    scale = D ** -0.5

    def q_block(q_b):
        s = jnp.einsum('bqhgd,bkhd->bhgqk', q_b, k).astype(jnp.float32) * scale
        p = jax.nn.softmax(s, axis=-1).astype(v.dtype)
        return jnp.einsum('bhgqk,bkhd->bqhgd', p, v)

    out = lax.map(q_block, qg)
    return out.transpose(1, 0, 2, 3, 4, 5).reshape(B, S, GQA_HEADS * D)


def encoder_layer(x, p_l, norm_g, w_in, na_rpb, q_a_norm, w_q_b, kv_a_norm, w_kv_b,
                  gqa_q_norm, gqa_k_norm, w_out, ple_norm, w_pe, w_pg, pos_tabs):
    B, S, _ = x.shape
    cos_t, sin_t, cos_r, sin_r, cos_c, sin_c = pos_tabs
    h = rms_norm(x, norm_g)
    proj = h @ w_in
    cuts = np.cumsum(IN_SPLITS)[:-1].tolist()
    (na_q, na_k, na_v, na_g, c_q, c_kv, k_rope, mla_g,
     g_q, g_k, g_v, gqa_g) = jnp.split(proj, cuts, axis=-1)

    shp = (B, S, NA_HEADS, NA_HEAD_DIM)
    y_na = neighborhood_attention(na_q.reshape(shp), na_k.reshape(shp), na_v.reshape(shp), na_rpb)
    y_na = y_na * jax.nn.silu(na_g)

    q = (rms_norm(c_q, q_a_norm) @ w_q_b).reshape(B, S, MLA_HEADS, MLA_NOPE + MLA_ROPE)
    q_nope, q_rope = q[..., :MLA_NOPE], q[..., MLA_NOPE:]
    q_rope = apply_rope(q_rope, cos_t[:, None, :], sin_t[:, None, :])
    kv = (rms_norm(c_kv, kv_a_norm) @ w_kv_b).reshape(B, S, MLA_HEADS, MLA_NOPE + MLA_V)
    k_nope, v_mla = kv[..., :MLA_NOPE], kv[..., MLA_NOPE:]
    k_rope = apply_rope(k_rope, cos_t, sin_t)
    y_mla = mla_attention(q_nope, q_rope, k_nope, k_rope, v_mla) * jax.nn.silu(mla_g)

    half = GQA_HEAD_DIM // 2
    def axial(t):
        return jnp.concatenate([
            apply_rope(t[..., :half], cos_r[:, None, :], sin_r[:, None, :]),
            apply_rope(t[..., half:], cos_c[:, None, :], sin_c[:, None, :])], axis=-1)
    qg = axial(rms_norm(g_q.reshape(B, S, GQA_HEADS, GQA_HEAD_DIM), gqa_q_norm))
    kg = axial(rms_norm(g_k.reshape(B, S, GQA_KV_HEADS, GQA_HEAD_DIM), gqa_k_norm))
    vg = g_v.reshape(B, S, GQA_KV_HEADS, GQA_HEAD_DIM)
    y_gqa = gqa_attention(qg, kg, vg) * jax.nn.silu(gqa_g)

    mixed = jnp.concatenate([y_na, y_mla, y_gqa], axis=-1) @ w_out
    x = x + mixed

    gate = jax.nn.sigmoid(rms_norm(x, ple_norm) @ w_pg)
    return x + (p_l @ w_pe) * gate


def run_trunk(x, p, norm_g, w_in, na_rpb, mla_q_a_norm, mla_w_q_b, mla_kv_a_norm, mla_w_kv_b,
              gqa_q_norm, gqa_k_norm, w_out, ple_norm, w_pe, w_pg, final_norm):
    S = x.shape[1]
    t = jnp.arange(S)
    row, col = t // GRID_W, t % GRID_W
    cos_t, sin_t = rope_tables(t, MLA_ROPE, x.dtype)
    cos_r, sin_r = rope_tables(row, GQA_HEAD_DIM // 2, x.dtype)
    cos_c, sin_c = rope_tables(col, GQA_HEAD_DIM // 2, x.dtype)
    pos_tabs = (cos_t, sin_t, cos_r, sin_r, cos_c, sin_c)
    for i in range(DEPTH):
        x = encoder_layer(x, p[i], norm_g[i], w_in[i], na_rpb[i], mla_q_a_norm[i], mla_w_q_b[i],
                          mla_kv_a_norm[i], mla_w_kv_b[i], gqa_q_norm[i], gqa_k_norm[i], w_out[i],
                          ple_norm[i], w_pe[i], w_pg[i], pos_tabs)
    return rms_norm(x, final_norm)


def setup_inputs(seed: int = 0) -> dict:
    key = jax.random.key(seed)
    ks = jax.random.split(key, 20)
    f32 = jnp.float32
    nrm = lambda k, shape, s: jax.random.normal(k, shape, f32) * s
    gain = lambda k, shape: 1.0 + 0.02 * jax.random.normal(k, shape, f32)
    return {
        "x_prompt": nrm(ks[0], (BATCH, SEQ, D_MODEL), 1.0),
        "x_sample": nrm(ks[1], (DEC_BATCH, DEC_SEQ, D_MODEL), 1.0),
        "p_prompt": nrm(ks[2], (DEPTH, BATCH, SEQ, PLE_DIM), 1.0),
        "p_sample": nrm(ks[3], (DEPTH, DEC_BATCH, DEC_SEQ, PLE_DIM), 1.0),
        "norm_g": gain(ks[4], (DEPTH, D_MODEL)),
        "w_in": nrm(ks[5], (DEPTH, D_MODEL, IN_COLS), D_MODEL ** -0.5),
        "na_rpb": nrm(ks[6], (DEPTH, NA_HEADS, 2 * NA_WIN_H - 1, 2 * NA_WIN_W - 1), 0.1),
        "mla_q_a_norm": gain(ks[7], (DEPTH, MLA_Q_LORA)),
        "mla_w_q_b": nrm(ks[8], (DEPTH, MLA_Q_LORA, MLA_HEADS * (MLA_NOPE + MLA_ROPE)), MLA_Q_LORA ** -0.5),
        "mla_kv_a_norm": gain(ks[9], (DEPTH, MLA_KV_LORA)),
        "mla_w_kv_b": nrm(ks[10], (DEPTH, MLA_KV_LORA, MLA_HEADS * (MLA_NOPE + MLA_V)), MLA_KV_LORA ** -0.5),
        "gqa_q_norm": gain(ks[11], (DEPTH, GQA_HEAD_DIM)),
        "gqa_k_norm": gain(ks[12], (DEPTH, GQA_HEAD_DIM)),
        "w_out": nrm(ks[13], (DEPTH, MIX_WIDTH, D_MODEL), MIX_WIDTH ** -0.5),
        "ple_norm": gain(ks[14], (DEPTH, D_MODEL)),
        "w_pe": nrm(ks[15], (DEPTH, PLE_DIM, D_MODEL), PLE_DIM ** -0.5),
        "w_pg": nrm(ks[16], (DEPTH, D_MODEL, D_MODEL), D_MODEL ** -0.5),
        "final_norm": gain(ks[17], (D_MODEL,)),
    }


def reference(x_prompt, x_sample, p_prompt, p_sample, norm_g, w_in, na_rpb, mla_q_a_norm, mla_w_q_b,
              mla_kv_a_norm, mla_w_kv_b, gqa_q_norm, gqa_k_norm, w_out, ple_norm, w_pe, w_pg, final_norm):
    y_prompt = run_trunk(x_prompt, p_prompt, norm_g, w_in, na_rpb, mla_q_a_norm, mla_w_q_b, mla_kv_a_norm,
                         mla_w_kv_b, gqa_q_norm, gqa_k_norm, w_out, ple_norm, w_pe, w_pg, final_norm)
    y_sample = run_trunk(x_sample, p_sample, norm_g, w_in, na_rpb, mla_q_a_norm, mla_w_q_b, mla_kv_a_norm,
                         mla_w_kv_b, gqa_q_norm, gqa_k_norm, w_out, ple_norm, w_pe, w_pg, final_norm)
    return (y_prompt, y_sample)
```

```python
import functools

import numpy as np
import jax
import jax.numpy as jnp
from jax import lax
from jax.experimental import pallas as pl
from jax.experimental.pallas import tpu as pltpu

F32 = jnp.float32
BF16 = jnp.bfloat16

D_MODEL = 2048
GRID_W = 64
PLE_DIM = 256
EPS = 1e-6
ROPE_THETA = 10000.0

NA_HEADS, NA_HEAD_DIM, NA_WIN_H, NA_WIN_W = 8, 64, 8, 16
NA_WIDTH = NA_HEADS * NA_HEAD_DIM
MLA_HEADS, MLA_NOPE, MLA_ROPE, MLA_V = 6, 128, 64, 128
MLA_Q_LORA, MLA_KV_LORA = 512, 256
MLA_WIDTH = MLA_HEADS * MLA_V
MLA_QK = MLA_NOPE + MLA_ROPE
MLA_QK_PAD = 256
GQA_HEADS, GQA_KV_HEADS, GQA_HEAD_DIM = 12, 4, 64
GQA_GROUP = GQA_HEADS // GQA_KV_HEADS
GQA_WIDTH = GQA_HEADS * GQA_HEAD_DIM
GQA_KV_WIDTH = GQA_KV_HEADS * GQA_HEAD_DIM

VMEM_LIMIT_BYTES = 56 * 1024 * 1024

TM_PROJ = 512
TM_OUT = 256
TQ_FLASH = 512
TK_FLASH = 512
NA_ROWS_PER_STEP = 2
NA_KEY_ROWS = 10
NA_KEYS = NA_KEY_ROWS * GRID_W
NA_QT = NA_ROWS_PER_STEP * GRID_W
NA_CLASSES = 5
NA_TQ = 1024
NEG_BIG = -1e30

_NT = (((1,), (1,)), ((), ()))


def _dot(a, b):
    return jnp.dot(a, b, preferred_element_type=F32)


def _dot_nt(a, b):
    return lax.dot_general(a, b, _NT, preferred_element_type=F32)


def _params(n_axes):
    return pltpu.CompilerParams(
        dimension_semantics=("arbitrary",) * n_axes,
        vmem_limit_bytes=VMEM_LIMIT_BYTES,
    )


def _const_spec(shape):
    nd = len(shape)
    return pl.BlockSpec(shape, lambda *_: (0,) * nd, pipeline_mode=pl.Buffered(1))


def _silu(x):
    return x * (1.0 / (1.0 + jnp.exp(-x)))


def _rms_rows(x, g_row):
    ms = jnp.mean(x * x, axis=-1, keepdims=True)
    return x * lax.rsqrt(ms + EPS) * g_row


def _rms_cols(xt, g_col):
    ms = jnp.mean(xt * xt, axis=0, keepdims=True)
    return xt * lax.rsqrt(ms + EPS) * g_col


def _rot_half_cols(xt, n):
    return jnp.concatenate([-xt[n:2 * n], xt[0:n]], axis=0)


def _proj_na_kernel(x_ref, g_ref, w_ref, qt_ref, k_ref, vt_ref, gt_ref):
    h = _rms_rows(x_ref[...], g_ref[...]).astype(BF16)
    W = NA_WIDTH
    qt_ref[...] = _dot_nt(w_ref[0:W, :], h).astype(BF16)
    kt = _dot_nt(w_ref[W:2 * W, :], h)
    for p in range(NA_HEADS // 2):
        k_ref[p] = kt[128 * p:128 * (p + 1), :].T.astype(BF16)
    vt_ref[...] = _dot_nt(w_ref[2 * W:3 * W, :], h).astype(BF16)
    gt_ref[...] = _silu(_dot_nt(w_ref[3 * W:4 * W, :], h))


def _proj_na(x, norm_g, wt):
    T = x.shape[0]
    tm = TM_PROJ
    return pl.pallas_call(
        _proj_na_kernel,
        grid=(T // tm,),
        in_specs=[
            pl.BlockSpec((tm, D_MODEL), lambda i: (i, 0)),
            _const_spec((1, D_MODEL)),
            _const_spec((4 * NA_WIDTH, D_MODEL)),
        ],
        out_specs=[
            pl.BlockSpec((NA_WIDTH, tm), lambda i: (0, i)),
            pl.BlockSpec((NA_HEADS // 2, tm, 128), lambda i: (0, i, 0)),
            pl.BlockSpec((NA_WIDTH, tm), lambda i: (0, i)),
            pl.BlockSpec((NA_WIDTH, tm), lambda i: (0, i)),
        ],
        out_shape=[
            jax.ShapeDtypeStruct((NA_WIDTH, T), BF16),
            jax.ShapeDtypeStruct((NA_HEADS // 2, T, 128), BF16),
            jax.ShapeDtypeStruct((NA_WIDTH, T), BF16),
            jax.ShapeDtypeStruct((NA_WIDTH, T), F32),
        ],
        compiler_params=_params(1),
        name="proj_na",
    )(x, norm_g, wt)


def _proj_gqa_kernel(x_ref, g_ref, w_ref, qn_ref, kn_ref, tab_ref, qt_ref, k_ref, vt_ref, gt_ref):
    h = _rms_rows(x_ref[...], g_ref[...]).astype(BF16)
    cos = tab_ref[0:GQA_HEAD_DIM, :]
    sin = tab_ref[GQA_HEAD_DIM:2 * GQA_HEAD_DIM, :]
    q4 = GQA_HEAD_DIM // 4

    def norm_rope(xt, g_col):
        y = _rms_cols(xt, g_col)
        rot = jnp.concatenate([-y[q4:2 * q4], y[0:q4], -y[3 * q4:4 * q4], y[2 * q4:3 * q4]], axis=0)
        return y * cos + rot * sin

    o = 0
    qt = _dot_nt(w_ref[o:o + GQA_WIDTH, :], h)
    for hd in range(GQA_HEADS):
        r = slice(GQA_HEAD_DIM * hd, GQA_HEAD_DIM * (hd + 1))
        qt_ref[r, :] = (norm_rope(qt[r, :], qn_ref[...]) * (GQA_HEAD_DIM ** -0.5)).astype(BF16)
    o += GQA_WIDTH
    kt = _dot_nt(w_ref[o:o + GQA_KV_WIDTH, :], h)
    for p in range(GQA_KV_HEADS // 2):
        pair = jnp.concatenate(
            [norm_rope(kt[GQA_HEAD_DIM * (2 * p + e):GQA_HEAD_DIM * (2 * p + e + 1), :], kn_ref[...])
             for e in range(2)], axis=0)
        k_ref[p] = pair.T.astype(BF16)
    o += GQA_KV_WIDTH
    vt_ref[...] = _dot_nt(w_ref[o:o + GQA_KV_WIDTH, :], h).astype(BF16)
    o += GQA_KV_WIDTH
    gt_ref[...] = _silu(_dot_nt(w_ref[o:o + GQA_WIDTH, :], h))


def _proj_gqa(x, norm_g, wt, qn_col, kn_col, tabs, S):
    T = x.shape[0]
    tm = TM_PROJ
    nb = S // tm
    rows = 2 * GQA_WIDTH + 2 * GQA_KV_WIDTH
    return pl.pallas_call(
        _proj_gqa_kernel,
        grid=(T // tm,),
        in_specs=[
            pl.BlockSpec((tm, D_MODEL), lambda i: (i, 0)),
            _const_spec((1, D_MODEL)),
            _const_spec((rows, D_MODEL)),
            _const_spec((GQA_HEAD_DIM, 1)),
            _const_spec((GQA_HEAD_DIM, 1)),
            pl.BlockSpec((2 * GQA_HEAD_DIM, tm), lambda i: (0, i % nb)),
        ],
        out_specs=[
            pl.BlockSpec((GQA_WIDTH, tm), lambda i: (0, i)),
            pl.BlockSpec((GQA_KV_HEADS // 2, tm, 128), lambda i: (0, i, 0)),
            pl.BlockSpec((GQA_KV_WIDTH, tm), lambda i: (0, i)),
            pl.BlockSpec((GQA_WIDTH, tm), lambda i: (0, i)),
        ],
        out_shape=[
            jax.ShapeDtypeStruct((GQA_WIDTH, T), BF16),
            jax.ShapeDtypeStruct((GQA_KV_HEADS // 2, T, 128), BF16),
            jax.ShapeDtypeStruct((GQA_KV_WIDTH, T), BF16),
            jax.ShapeDtypeStruct((GQA_WIDTH, T), F32),
        ],
        compiler_params=_params(1),
        name="proj_gqa",
    )(x, norm_g, wt, qn_col, kn_col, tabs)


def _proj_mla_kernel(x_ref, g_ref, w_ref, qan_ref, wqb_ref, kvan_ref, wkvb_ref, tab_ref,
                     qt_ref, k_ref, vt_ref, gt_ref):
    h = _rms_rows(x_ref[...], g_ref[...]).astype(BF16)
    cos = tab_ref[0:MLA_ROPE, :]
    sin = tab_ref[MLA_ROPE:2 * MLA_ROPE, :]
    tm = h.shape[0]
    half = MLA_ROPE // 2
    scale = MLA_QK ** -0.5

    def rope(xt):
        return xt * cos + _rot_half_cols(xt, half) * sin

    o = 0
    cq = _rms_cols(_dot_nt(w_ref[o:o + MLA_Q_LORA, :], h), qan_ref[...]).astype(BF16)
    qt = _dot(wqb_ref[...], cq)
    zpad = jnp.zeros((MLA_QK_PAD - MLA_QK, tm), BF16)
    for hd in range(MLA_HEADS):
        b = MLA_QK * hd
        ob = MLA_QK_PAD * hd
        qt_ref[ob:ob + MLA_NOPE, :] = (qt[b:b + MLA_NOPE, :] * scale).astype(BF16)
        qt_ref[ob + MLA_NOPE:ob + MLA_QK, :] = (rope(qt[b + MLA_NOPE:b + MLA_QK, :]) * scale).astype(BF16)
        qt_ref[ob + MLA_QK:ob + MLA_QK_PAD, :] = zpad
    o += MLA_Q_LORA
    ckv = _rms_cols(_dot_nt(w_ref[o:o + MLA_KV_LORA, :], h), kvan_ref[...]).astype(BF16)
    kvt = _dot(wkvb_ref[...], ckv)
    o += MLA_KV_LORA
    krt = rope(_dot_nt(w_ref[o:o + MLA_ROPE, :], h))
    kr = jnp.concatenate([krt, jnp.zeros_like(krt)], axis=0).T.astype(BF16)
    o += MLA_ROPE
    for hd in range(MLA_HEADS):
        b = (MLA_NOPE + MLA_V) * hd
        k_ref[hd, :, 0:MLA_NOPE] = kvt[b:b + MLA_NOPE, :].T.astype(BF16)
        k_ref[hd, :, MLA_NOPE:MLA_QK_PAD] = kr
        vt_ref[MLA_V * hd:MLA_V * (hd + 1), :] = kvt[b + MLA_NOPE:b + MLA_NOPE + MLA_V, :].astype(BF16)
    gt_ref[...] = _silu(_dot_nt(w_ref[o:o + MLA_WIDTH, :], h))


def _proj_mla(x, norm_g, wt, qan_col, wqb_t, kvan_col, wkvb_t, tabs, S):
    T = x.shape[0]
    tm = TM_PROJ
    nb = S // tm
    rows = MLA_Q_LORA + MLA_KV_LORA + MLA_ROPE + MLA_WIDTH
    return pl.pallas_call(
        _proj_mla_kernel,
        grid=(T // tm,),
        in_specs=[
            pl.BlockSpec((tm, D_MODEL), lambda i: (i, 0)),
            _const_spec((1, D_MODEL)),
            _const_spec((rows, D_MODEL)),
            _const_spec((MLA_Q_LORA, 1)),
            _const_spec((MLA_HEADS * MLA_QK, MLA_Q_LORA)),
            _const_spec((MLA_KV_LORA, 1)),
            _const_spec((MLA_HEADS * (MLA_NOPE + MLA_V), MLA_KV_LORA)),
            pl.BlockSpec((2 * MLA_ROPE, tm), lambda i: (0, i % nb)),
        ],
        out_specs=[
            pl.BlockSpec((MLA_HEADS * MLA_QK_PAD, tm), lambda i: (0, i)),
            pl.BlockSpec((MLA_HEADS, tm, MLA_QK_PAD), lambda i: (0, i, 0)),
            pl.BlockSpec((MLA_WIDTH, tm), lambda i: (0, i)),
            pl.BlockSpec((MLA_WIDTH, tm), lambda i: (0, i)),
        ],
        out_shape=[
            jax.ShapeDtypeStruct((MLA_HEADS * MLA_QK_PAD, T), BF16),
            jax.ShapeDtypeStruct((MLA_HEADS, T, MLA_QK_PAD), BF16),
            jax.ShapeDtypeStruct((MLA_WIDTH, T), BF16),
            jax.ShapeDtypeStruct((MLA_WIDTH, T), F32),
        ],
        compiler_params=_params(1),
        name="proj_mla",
    )(x, norm_g, wt, qan_col, wqb_t, kvan_col, wkvb_t, tabs)


def _flash_kernel(qt_ref, k_ref, vt_ref, gt_ref, o_ref, acc_ref, *, n_chunks, tk, q_half_of):
    g = pl.program_id(1)
    qt = qt_ref[...]
    if q_half_of is not None:
        half = q_half_of(g)
        z = jnp.zeros_like(qt)
        qt = jnp.concatenate([jnp.where(half == 0, qt, z), jnp.where(half == 1, qt, z)], axis=0)
    tq = qt.shape[1]
    acc_ref[...] = jnp.zeros_like(acc_ref)

    def body(c, carry):
        m, l = carry
        start = pl.multiple_of(c * tk, tk)
        s = _dot(k_ref[0, pl.ds(start, tk), :], qt)
        m_new = jnp.maximum(m, jnp.max(s, axis=0, keepdims=True))
        alpha = jnp.exp(m - m_new)
        p = jnp.exp(s - m_new)
        l = alpha * l + jnp.sum(p, axis=0, keepdims=True)
        acc_ref[...] = alpha * acc_ref[...] + _dot(vt_ref[:, pl.ds(start, tk)], p.astype(BF16))
        return m_new, l

    m0 = jnp.full((1, tq), -jnp.inf, F32)
    l0 = jnp.zeros((1, tq), F32)
    _, l = lax.fori_loop(0, n_chunks, body, (m0, l0))
    o_ref[...] = (acc_ref[...] * (1.0 / l) * gt_ref[...]).astype(BF16)


def _flash(qt, k, vt, gt, *, B, S, heads, dk, dv, kv_of, q_half_of=None):
    tq, tk = TQ_FLASH, min(TK_FLASH, S)
    nq = S // tq
    dkk = k.shape[2]
    kern = functools.partial(_flash_kernel, n_chunks=S // tk, tk=tk, q_half_of=q_half_of)
    k_of, v_of = kv_of
    return pl.pallas_call(
        kern,
        grid=(B, heads, nq),
        in_specs=[
            pl.BlockSpec((dk, tq), lambda b, g, i: (g, b * nq + i)),
            pl.BlockSpec((1, S, dkk), lambda b, g, i: (k_of(g), b, 0)),
            pl.BlockSpec((dv, S), lambda b, g, i: (v_of(g), b)),
            pl.BlockSpec((dv, tq), lambda b, g, i: (g, b * nq + i)),
        ],
        out_specs=pl.BlockSpec((dv, tq), lambda b, g, i: (g, b * nq + i)),
        out_shape=jax.ShapeDtypeStruct((heads * dv, B * S), BF16),
        scratch_shapes=[pltpu.VMEM((dv, tq), F32)],
        compiler_params=_params(3),
        name="flash_dk%d" % dk,
    )(qt, k, vt, gt)


def _na_kernel(qt_ref, k_ref, vt_ref, bias_ref, gt_ref, o_ref, *, rows, n_inner):
    hd = pl.program_id(1)
    i = pl.program_id(2)
    half = hd % 2
    half_rows = rows // NA_ROWS_PER_STEP

    def body(t, _):
        j = i * n_inner + t
        qs = pl.multiple_of(t * NA_QT, NA_QT)
        q = qt_ref[:, pl.ds(qs, NA_QT)]
        z = jnp.zeros_like(q)
        q2 = jnp.concatenate([jnp.where(half == 0, q, z), jnp.where(half == 1, q, z)], axis=0)
        kb = jnp.clip(NA_ROWS_PER_STEP * j - NA_WIN_H // 2, 0, rows - NA_KEY_ROWS)
        ks = pl.multiple_of(kb * GRID_W, 2 * GRID_W)
        cls = jnp.where(j < 2, j, jnp.where(j > half_rows - 3, j - (half_rows - NA_CLASSES), 2))
        s = _dot(k_ref[0, pl.ds(ks, NA_KEYS), :], q2) + bias_ref[0, cls]
        m = jnp.max(s, axis=0, keepdims=True)
        p = jnp.exp(s - m)
        l = jnp.sum(p, axis=0, keepdims=True)
        y = _dot(vt_ref[:, pl.ds(ks, NA_KEYS)], p.astype(BF16))
        o_ref[:, pl.ds(qs, NA_QT)] = (y * (1.0 / l) * gt_ref[:, pl.ds(qs, NA_QT)]).astype(BF16)
        return 0

    lax.fori_loop(0, n_inner, body, 0)


def _na_attn(qt, k, vt, bias, gt, *, B, S):
    rows = S // GRID_W
    tq = min(NA_TQ, S)
    nq = S // tq
    n_inner = tq // NA_QT
    kern = functools.partial(_na_kernel, rows=rows, n_inner=n_inner)
    D = NA_HEAD_DIM
    return pl.pallas_call(
        kern,
        grid=(B, NA_HEADS, nq),
        in_specs=[
            pl.BlockSpec((D, tq), lambda b, h, i: (h, b * nq + i)),
            pl.BlockSpec((1, S, 2 * D), lambda b, h, i: (h // 2, b, 0)),
            pl.BlockSpec((D, S), lambda b, h, i: (h, b)),
            pl.BlockSpec((1, NA_CLASSES, NA_KEYS, NA_QT), lambda b, h, i: (h, 0, 0, 0)),
            pl.BlockSpec((D, tq), lambda b, h, i: (h, b * nq + i)),
        ],
        out_specs=pl.BlockSpec((D, tq), lambda b, h, i: (h, b * nq + i)),
        out_shape=jax.ShapeDtypeStruct((NA_WIDTH, B * S), BF16),
        compiler_params=_params(3),
        name="na_attn",
    )(qt, k, vt, bias, gt)


def _na_bias_table(rpb, rows):
    hr = rows // NA_ROWS_PER_STEP
    reps = [0, 1, 2, hr - 2, hr - 1]
    i = np.arange(NA_KEY_ROWS)[:, None, None, None]
    kc = np.arange(GRID_W)[None, :, None, None]
    rr = np.arange(NA_ROWS_PER_STEP)[None, None, :, None]
    qc = np.arange(GRID_W)[None, None, None, :]
    cs = np.clip(qc - NA_WIN_W // 2, 0, GRID_W - NA_WIN_W)
    col_ok = (kc >= cs) & (kc < cs + NA_WIN_W)
    dj = kc - qc + (NA_WIN_W - 1)
    di_all, ok_all = [], []
    for j in reps:
        r = NA_ROWS_PER_STEP * j + rr
        rs = np.clip(r - NA_WIN_H // 2, 0, rows - NA_WIN_H)
        kb = np.clip(NA_ROWS_PER_STEP * j - NA_WIN_H // 2, 0, rows - NA_KEY_ROWS)
        kr = kb + i
        row_ok = (kr >= rs) & (kr < rs + NA_WIN_H)
        di = kr - r + (NA_WIN_H - 1)
        ok = np.broadcast_to(row_ok & col_ok, (NA_KEY_ROWS, GRID_W, NA_ROWS_PER_STEP, GRID_W))
        di_all.append(np.broadcast_to(np.clip(di, 0, 2 * NA_WIN_H - 2), ok.shape))
        ok_all.append(ok)
    di = np.stack(di_all).reshape(NA_CLASSES, NA_KEYS, NA_QT)
    ok = np.stack(ok_all).reshape(NA_CLASSES, NA_KEYS, NA_QT)
    djb = np.broadcast_to(np.clip(dj, 0, 2 * NA_WIN_W - 2), (NA_CLASSES,) + ok_all[0].shape)
    djb = djb.reshape(NA_CLASSES, NA_KEYS, NA_QT)
    vals = rpb[:, di, djb]
    return jnp.where(ok[None], vals, NEG_BIG).astype(F32)


def _out_kernel(zna_ref, zmla_ref, zgqa_ref, x_ref, p_ref, wna_ref, wmla_ref, wgqa_ref,
                pn_ref, wpg_ref, wpe_ref, fn_ref, o_ref, *, final):
    mixed_t = (_dot(wna_ref[...], zna_ref[...]) + _dot(wmla_ref[...], zmla_ref[...])
               + _dot(wgqa_ref[...], zgqa_ref[...]))
    x1 = x_ref[...] + mixed_t.T
    hn = _rms_rows(x1, pn_ref[...]).astype(BF16)
    gate = 1.0 / (1.0 + jnp.exp(-_dot(hn, wpg_ref[...])))
    pe = _dot(p_ref[...].astype(BF16), wpe_ref[...])
    x2 = x1 + pe * gate
    if final:
        x2 = _rms_rows(x2, fn_ref[...])
    o_ref[...] = x2


def _out_ple(zna, zmla, zgqa, x, p, wo_na, wo_mla, wo_gqa, ple_norm, wpg, wpe, final_norm, *, final):
    T = x.shape[0]
    tm = TM_OUT
    return pl.pallas_call(
        functools.partial(_out_kernel, final=final),
        grid=(T // tm,),
        in_specs=[
            pl.BlockSpec((NA_WIDTH, tm), lambda i: (0, i)),
            pl.BlockSpec((MLA_WIDTH, tm), lambda i: (0, i)),
            pl.BlockSpec((GQA_WIDTH, tm), lambda i: (0, i)),
            pl.BlockSpec((tm, D_MODEL), lambda i: (i, 0)),
            pl.BlockSpec((tm, PLE_DIM), lambda i: (i, 0)),
            _const_spec((D_MODEL, NA_WIDTH)),
            _const_spec((D_MODEL, MLA_WIDTH)),
            _const_spec((D_MODEL, GQA_WIDTH)),
            _const_spec((1, D_MODEL)),
            _const_spec((D_MODEL, D_MODEL)),
            _const_spec((PLE_DIM, D_MODEL)),
            _const_spec((1, D_MODEL)),
        ],
        out_specs=pl.BlockSpec((tm, D_MODEL), lambda i: (i, 0)),
        out_shape=jax.ShapeDtypeStruct((T, D_MODEL), F32),
        compiler_params=_params(1),
        name="out_ple",
    )(zna, zmla, zgqa, x, p, wo_na, wo_mla, wo_gqa, ple_norm, wpg, wpe, final_norm)


def _rope_tables_t(pos, dim):
    inv = ROPE_THETA ** (-jnp.arange(0, dim, 2, dtype=F32) / dim)
    ang = pos.astype(F32)[:, None] * inv[None, :]
    ang = jnp.concatenate([ang, ang], axis=-1)
    return jnp.cos(ang).T, jnp.sin(ang).T


def _tables(S):
    t = jnp.arange(S)
    cos_t, sin_t = _rope_tables_t(t, MLA_ROPE)
    cos_r, sin_r = _rope_tables_t(t // GRID_W, GQA_HEAD_DIM // 2)
    cos_c, sin_c = _rope_tables_t(t % GRID_W, GQA_HEAD_DIM // 2)
    return (jnp.concatenate([cos_t, sin_t], axis=0),
            jnp.concatenate([cos_r, cos_c, sin_r, sin_c], axis=0))


def _layer_params(i, norm_g, w_in, na_rpb, mla_q_a_norm, mla_w_q_b, mla_kv_a_norm, mla_w_kv_b,
                  gqa_q_norm, gqa_k_norm, w_out, ple_norm, w_pe, w_pg):
    wt = w_in[i].T
    o = 0
    w_na = wt[o:o + 4 * NA_WIDTH]
    w_na = jnp.concatenate([w_na[:NA_WIDTH] * (NA_HEAD_DIM ** -0.5), w_na[NA_WIDTH:]], axis=0)
    o += 4 * NA_WIDTH
    n_mla = MLA_Q_LORA + MLA_KV_LORA + MLA_ROPE + MLA_WIDTH
    w_mla = wt[o:o + n_mla]
    o += n_mla
    w_gqa = wt[o:]
    wo_t = w_out[i].T
    return dict(
        norm_g=norm_g[i][None, :],
        w_na=w_na.astype(BF16), w_mla=w_mla.astype(BF16), w_gqa=w_gqa.astype(BF16),
        qan=mla_q_a_norm[i][:, None], wqb_t=mla_w_q_b[i].T.astype(BF16),
        kvan=mla_kv_a_norm[i][:, None], wkvb_t=mla_w_kv_b[i].T.astype(BF16),
        qn=gqa_q_norm[i][:, None], kn=gqa_k_norm[i][:, None],
        rpb=na_rpb[i],
        wo_na=wo_t[:, :NA_WIDTH].astype(BF16),
        wo_mla=wo_t[:, NA_WIDTH:NA_WIDTH + MLA_WIDTH].astype(BF16),
        wo_gqa=wo_t[:, NA_WIDTH + MLA_WIDTH:].astype(BF16),
        ple_norm=ple_norm[i][None, :], wpg=w_pg[i].astype(BF16), wpe=w_pe[i].astype(BF16),
    )


def _run_trunk(x, p, layers, final_norm):
    B, S, _ = x.shape
    T = B * S
    rows = S // GRID_W
    assert S % TM_PROJ == 0 and S % TQ_FLASH == 0 and S % GRID_W == 0
    assert rows >= NA_KEY_ROWS and rows % NA_ROWS_PER_STEP == 0 and rows // NA_ROWS_PER_STEP >= NA_CLASSES
    tab_mla, tab_gqa = _tables(S)
    xf = x.reshape(T, D_MODEL)
    fn = final_norm[None, :]
    depth = len(layers)
    for i, L in enumerate(layers):
        pf = p[i].reshape(T, PLE_DIM)
        na_qt, na_k, na_vt, na_gt = _proj_na(xf, L["norm_g"], L["w_na"])
        ml_qt, ml_k, ml_vt, ml_gt = _proj_mla(xf, L["norm_g"], L["w_mla"], L["qan"], L["wqb_t"],
                                              L["kvan"], L["wkvb_t"], tab_mla, S)
        gq_qt, gq_k, gq_vt, gq_gt = _proj_gqa(xf, L["norm_g"], L["w_gqa"], L["qn"], L["kn"], tab_gqa, S)
        bias = _na_bias_table(L["rpb"], rows)
        z_na = _na_attn(na_qt, na_k, na_vt, bias, na_gt, B=B, S=S)
        z_mla = _flash(ml_qt, ml_k, ml_vt, ml_gt, B=B, S=S, heads=MLA_HEADS, dk=MLA_QK_PAD, dv=MLA_V,
                       kv_of=(lambda g: g, lambda g: g))
        z_gqa = _flash(gq_qt, gq_k, gq_vt, gq_gt, B=B, S=S, heads=GQA_HEADS, dk=GQA_HEAD_DIM, dv=GQA_HEAD_DIM,
                       kv_of=(lambda g: g // (2 * GQA_GROUP), lambda g: g // GQA_GROUP),
                       q_half_of=lambda g: (g // GQA_GROUP) % 2)
        xf = _out_ple(z_na, z_mla, z_gqa, xf, pf, L["wo_na"], L["wo_mla"], L["wo_gqa"], L["ple_norm"],
                      L["wpg"], L["wpe"], fn, final=(i == depth - 1))
    return xf.reshape(B, S, D_MODEL)


def kernel(x_prompt, x_sample, p_prompt, p_sample, norm_g, w_in, na_rpb, mla_q_a_norm, mla_w_q_b,
           mla_kv_a_norm, mla_w_kv_b, gqa_q_norm, gqa_k_norm, w_out, ple_norm, w_pe, w_pg, final_norm):
    depth = w_in.shape[0]
    layers = [_layer_params(i, norm_g, w_in, na_rpb, mla_q_a_norm, mla_w_q_b, mla_kv_a_norm, mla_w_kv_b,
                            gqa_q_norm, gqa_k_norm, w_out, ple_norm, w_pe, w_pg) for i in range(depth)]
    y_prompt = _run_trunk(x_prompt, p_prompt, layers, final_norm)
    y_sample = _run_trunk(x_sample, p_sample, layers, final_norm)
    return (y_prompt, y_sample)
```

```python
import functools

import numpy as np
import jax
import jax.numpy as jnp
from jax import lax
from jax.experimental import pallas as pl
from jax.experimental.pallas import tpu as pltpu

F32 = jnp.float32
BF16 = jnp.bfloat16

D_MODEL = 2048
GRID_W = 64
PLE_DIM = 256
EPS = 1e-6
ROPE_THETA = 10000.0

NA_HEADS, NA_HEAD_DIM, NA_WIN_H, NA_WIN_W = 8, 64, 8, 16
NA_WIDTH = NA_HEADS * NA_HEAD_DIM
MLA_HEADS, MLA_NOPE, MLA_ROPE, MLA_V = 6, 128, 64, 128
MLA_Q_LORA, MLA_KV_LORA = 512, 256
MLA_WIDTH = MLA_HEADS * MLA_V
MLA_QK = MLA_NOPE + MLA_ROPE
MLA_QK_PAD = 256
GQA_HEADS, GQA_KV_HEADS, GQA_HEAD_DIM = 12, 4, 64
GQA_GROUP = GQA_HEADS // GQA_KV_HEADS
GQA_WIDTH = GQA_HEADS * GQA_HEAD_DIM
GQA_KV_WIDTH = GQA_KV_HEADS * GQA_HEAD_DIM

VMEM_LIMIT_BYTES = 56 * 1024 * 1024

TM_PROJ = 512
TM_OUT = 256
TQ_FLASH = 512
TK_FLASH = 512
FLASH_UNROLL = 4
SUM_ROWS = 16
NA_ROWS_PER_STEP = 2
NA_KEY_ROWS = 10
NA_KEYS = NA_KEY_ROWS * GRID_W
NA_QT = NA_ROWS_PER_STEP * GRID_W
NA_CLASSES = 5
NA_TQ = 1024
NEG_BIG = -1e30
LOG2E = 1.4426950408889634

_NT = (((1,), (1,)), ((), ()))


def _dot(a, b):
    return jnp.dot(a, b, preferred_element_type=F32)


def _dot_nt(a, b):
    return lax.dot_general(a, b, _NT, preferred_element_type=F32)


def _params(n_axes):
    return pltpu.CompilerParams(
        dimension_semantics=("arbitrary",) * n_axes,
        vmem_limit_bytes=VMEM_LIMIT_BYTES,
    )


def _const_spec(shape):
    nd = len(shape)
    return pl.BlockSpec(shape, lambda *_: (0,) * nd, pipeline_mode=pl.Buffered(1))


def _silu(x):
    return x * (1.0 / (1.0 + jnp.exp(-x)))


def _rms_rows(x, g_row):
    ms = jnp.mean(x * x, axis=-1, keepdims=True)
    return x * lax.rsqrt(ms + EPS) * g_row


def _rms_cols(xt, g_col):
    ms = jnp.mean(xt * xt, axis=0, keepdims=True)
    return xt * lax.rsqrt(ms + EPS) * g_col


def _rot_half_cols(xt, n):
    return jnp.concatenate([-xt[n:2 * n], xt[0:n]], axis=0)


def _proj_na_kernel(x_ref, g_ref, w_ref, qt_ref, k_ref, vt_ref, gt_ref):
    h = _rms_rows(x_ref[...], g_ref[...]).astype(BF16)
    W = NA_WIDTH
    qt_ref[...] = _dot_nt(w_ref[0:W, :], h).astype(BF16)
    kt = _dot_nt(w_ref[W:2 * W, :], h)
    for p in range(NA_HEADS // 2):
        k_ref[p] = kt[128 * p:128 * (p + 1), :].T.astype(BF16)
    vt_ref[...] = _dot_nt(w_ref[2 * W:3 * W, :], h).astype(BF16)
    gt_ref[...] = _silu(_dot_nt(w_ref[3 * W:4 * W, :], h))


def _proj_na(x, norm_g, wt):
    T = x.shape[0]
    tm = TM_PROJ
    return pl.pallas_call(
        _proj_na_kernel,
        grid=(T // tm,),
        in_specs=[
            pl.BlockSpec((tm, D_MODEL), lambda i: (i, 0)),
            _const_spec((1, D_MODEL)),
            _const_spec((4 * NA_WIDTH, D_MODEL)),
        ],
        out_specs=[
            pl.BlockSpec((NA_WIDTH, tm), lambda i: (0, i)),
            pl.BlockSpec((NA_HEADS // 2, tm, 128), lambda i: (0, i, 0)),
            pl.BlockSpec((NA_WIDTH, tm), lambda i: (0, i)),
            pl.BlockSpec((NA_WIDTH, tm), lambda i: (0, i)),
        ],
        out_shape=[
            jax.ShapeDtypeStruct((NA_WIDTH, T), BF16),
            jax.ShapeDtypeStruct((NA_HEADS // 2, T, 128), BF16),
            jax.ShapeDtypeStruct((NA_WIDTH, T), BF16),
            jax.ShapeDtypeStruct((NA_WIDTH, T), F32),
        ],
        compiler_params=_params(1),
        name="proj_na",
    )(x, norm_g, wt)


def _proj_gqa_kernel(x_ref, g_ref, w_ref, qn_ref, kn_ref, tab_ref, qt_ref, k_ref, vt_ref, gt_ref):
    h = _rms_rows(x_ref[...], g_ref[...]).astype(BF16)
    cos = tab_ref[0:GQA_HEAD_DIM, :]
    sin = tab_ref[GQA_HEAD_DIM:2 * GQA_HEAD_DIM, :]
    q4 = GQA_HEAD_DIM // 4

    def norm_rope(xt, g_col):
        y = _rms_cols(xt, g_col)
        rot = jnp.concatenate([-y[q4:2 * q4], y[0:q4], -y[3 * q4:4 * q4], y[2 * q4:3 * q4]], axis=0)
        return y * cos + rot * sin

    o = 0
    qt = _dot_nt(w_ref[o:o + GQA_WIDTH, :], h)
    for hd in range(GQA_HEADS):
        r = slice(GQA_HEAD_DIM * hd, GQA_HEAD_DIM * (hd + 1))
        qt_ref[r, :] = (norm_rope(qt[r, :], qn_ref[...]) * (GQA_HEAD_DIM ** -0.5 * LOG2E)).astype(BF16)
    o += GQA_WIDTH
    kt = _dot_nt(w_ref[o:o + GQA_KV_WIDTH, :], h)
    for p in range(GQA_KV_HEADS // 2):
        pair = jnp.concatenate(
            [norm_rope(kt[GQA_HEAD_DIM * (2 * p + e):GQA_HEAD_DIM * (2 * p + e + 1), :], kn_ref[...])
             for e in range(2)], axis=0)
        k_ref[p] = pair.T.astype(BF16)
    o += GQA_KV_WIDTH
    vt_ref[...] = _dot_nt(w_ref[o:o + GQA_KV_WIDTH, :], h).astype(BF16)
    o += GQA_KV_WIDTH
    gt_ref[...] = _silu(_dot_nt(w_ref[o:o + GQA_WIDTH, :], h))


def _proj_gqa(x, norm_g, wt, qn_col, kn_col, tabs, S):
    T = x.shape[0]
    tm = TM_PROJ
    nb = S // tm
    rows = 2 * GQA_WIDTH + 2 * GQA_KV_WIDTH
    return pl.pallas_call(
        _proj_gqa_kernel,
        grid=(T // tm,),
        in_specs=[
            pl.BlockSpec((tm, D_MODEL), lambda i: (i, 0)),
            _const_spec((1, D_MODEL)),
            _const_spec((rows, D_MODEL)),
            _const_spec((GQA_HEAD_DIM, 1)),
            _const_spec((GQA_HEAD_DIM, 1)),
            pl.BlockSpec((2 * GQA_HEAD_DIM, tm), lambda i: (0, i % nb)),
        ],
        out_specs=[
            pl.BlockSpec((GQA_WIDTH, tm), lambda i: (0, i)),
            pl.BlockSpec((GQA_KV_HEADS // 2, tm, 128), lambda i: (0, i, 0)),
            pl.BlockSpec((GQA_KV_WIDTH, tm), lambda i: (0, i)),
            pl.BlockSpec((GQA_WIDTH, tm), lambda i: (0, i)),
        ],
        out_shape=[
            jax.ShapeDtypeStruct((GQA_WIDTH, T), BF16),
            jax.ShapeDtypeStruct((GQA_KV_HEADS // 2, T, 128), BF16),
            jax.ShapeDtypeStruct((GQA_KV_WIDTH, T), BF16),
            jax.ShapeDtypeStruct((GQA_WIDTH, T), F32),
        ],
        compiler_params=_params(1),
        name="proj_gqa",
    )(x, norm_g, wt, qn_col, kn_col, tabs)


def _proj_mla_kernel(x_ref, g_ref, w_ref, qan_ref, wqb_ref, kvan_ref, wkvb_ref, tab_ref,
                     qt_ref, k_ref, vt_ref, gt_ref):
    h = _rms_rows(x_ref[...], g_ref[...]).astype(BF16)
    cos = tab_ref[0:MLA_ROPE, :]
    sin = tab_ref[MLA_ROPE:2 * MLA_ROPE, :]
    tm = h.shape[0]
    half = MLA_ROPE // 2
    scale = MLA_QK ** -0.5 * LOG2E

    def rope(xt):
        return xt * cos + _rot_half_cols(xt, half) * sin

    o = 0
    cq = _rms_cols(_dot_nt(w_ref[o:o + MLA_Q_LORA, :], h), qan_ref[...]).astype(BF16)
    qt = _dot(wqb_ref[...], cq)
    zpad = jnp.zeros((MLA_QK_PAD - MLA_QK, tm), BF16)
    for hd in range(MLA_HEADS):
        b = MLA_QK * hd
        ob = MLA_QK_PAD * hd
        qt_ref[ob:ob + MLA_NOPE, :] = (qt[b:b + MLA_NOPE, :] * scale).astype(BF16)
        qt_ref[ob + MLA_NOPE:ob + MLA_QK, :] = (rope(qt[b + MLA_NOPE:b + MLA_QK, :]) * scale).astype(BF16)
        qt_ref[ob + MLA_QK:ob + MLA_QK_PAD, :] = zpad
    o += MLA_Q_LORA
    ckv = _rms_cols(_dot_nt(w_ref[o:o + MLA_KV_LORA, :], h), kvan_ref[...]).astype(BF16)
    kvt = _dot(wkvb_ref[...], ckv)
    o += MLA_KV_LORA
    krt = rope(_dot_nt(w_ref[o:o + MLA_ROPE, :], h))
    kr = jnp.concatenate([krt, jnp.zeros_like(krt)], axis=0).T.astype(BF16)
    o += MLA_ROPE
    for hd in range(MLA_HEADS):
        b = (MLA_NOPE + MLA_V) * hd
        k_ref[hd, :, 0:MLA_NOPE] = kvt[b:b + MLA_NOPE, :].T.astype(BF16)
        k_ref[hd, :, MLA_NOPE:MLA_QK_PAD] = kr
        vt_ref[MLA_V * hd:MLA_V * (hd + 1), :] = kvt[b + MLA_NOPE:b + MLA_NOPE + MLA_V, :].astype(BF16)
    gt_ref[...] = _silu(_dot_nt(w_ref[o:o + MLA_WIDTH, :], h))


def _proj_mla(x, norm_g, wt, qan_col, wqb_t, kvan_col, wkvb_t, tabs, S):
    T = x.shape[0]
    tm = TM_PROJ
    nb = S // tm
    rows = MLA_Q_LORA + MLA_KV_LORA + MLA_ROPE + MLA_WIDTH
    return pl.pallas_call(
        _proj_mla_kernel,
        grid=(T // tm,),
        in_specs=[
            pl.BlockSpec((tm, D_MODEL), lambda i: (i, 0)),
            _const_spec((1, D_MODEL)),
            _const_spec((rows, D_MODEL)),
            _const_spec((MLA_Q_LORA, 1)),
            _const_spec((MLA_HEADS * MLA_QK, MLA_Q_LORA)),
            _const_spec((MLA_KV_LORA, 1)),
            _const_spec((MLA_HEADS * (MLA_NOPE + MLA_V), MLA_KV_LORA)),
            pl.BlockSpec((2 * MLA_ROPE, tm), lambda i: (0, i % nb)),
        ],
        out_specs=[
            pl.BlockSpec((MLA_HEADS * MLA_QK_PAD, tm), lambda i: (0, i)),
            pl.BlockSpec((MLA_HEADS, tm, MLA_QK_PAD), lambda i: (0, i, 0)),
            pl.BlockSpec((MLA_WIDTH, tm), lambda i: (0, i)),
            pl.BlockSpec((MLA_WIDTH, tm), lambda i: (0, i)),
        ],
        out_shape=[
            jax.ShapeDtypeStruct((MLA_HEADS * MLA_QK_PAD, T), BF16),
            jax.ShapeDtypeStruct((MLA_HEADS, T, MLA_QK_PAD), BF16),
            jax.ShapeDtypeStruct((MLA_WIDTH, T), BF16),
            jax.ShapeDtypeStruct((MLA_WIDTH, T), F32),
        ],
        compiler_params=_params(1),
        name="proj_mla",
    )(x, norm_g, wt, qan_col, wqb_t, kvan_col, wkvb_t, tabs)


def _flash_kernel(qt_ref, k_ref, vt_ref, gt_ref, o_ref, s_ref, acc_ref, *, n_chunks, tk, unroll, q_half_of):
    g = pl.program_id(1)
    qt = qt_ref[...]
    if q_half_of is not None:
        half = q_half_of(g)
        z = jnp.zeros_like(qt)
        qt = jnp.concatenate([jnp.where(half == 0, qt, z), jnp.where(half == 1, qt, z)], axis=0)
    tq = qt.shape[1]
    dv = vt_ref.shape[0]
    acc_ref[...] = jnp.zeros_like(acc_ref)
    ones = jnp.ones((SUM_ROWS, tk), BF16)

    def stage1(c, slot):
        start = pl.multiple_of(c * tk, tk)
        s = _dot(k_ref[0, pl.ds(start, tk), :], qt)
        s_ref[slot] = s
        return jnp.max(s, axis=0, keepdims=True)

    def stage2(c, slot, m, m_cur):
        start = pl.multiple_of(c * tk, tk)
        m_new = jnp.maximum(m, m_cur)
        alpha = jnp.exp2(m - m_new)
        p = jnp.exp2(s_ref[slot] - m_new).astype(BF16)
        v1 = jnp.concatenate([vt_ref[:, pl.ds(start, tk)], ones], axis=0)
        acc_ref[...] = alpha * acc_ref[...] + _dot(v1, p)
        return m_new

    def group(i, carry):
        m, m_cur = carry
        c = unroll * i
        for u in range(unroll):
            m_nxt = stage1(c + u + 1, (u + 1) % 2)
            m = stage2(c + u, u % 2, m, m_cur)
            m_cur = m_nxt
        return m, m_cur

    n_groups = (n_chunks - 1) // unroll
    m = jnp.full((1, tq), -jnp.inf, F32)
    m_cur = stage1(0, 0)
    m, m_cur = lax.fori_loop(0, n_groups, group, (m, m_cur))
    for c in range(n_groups * unroll, n_chunks):
        m_nxt = stage1(c + 1, (c + 1) % 2) if c + 1 < n_chunks else None
        m = stage2(c, c % 2, m, m_cur)
        m_cur = m_nxt
    l = acc_ref[dv:dv + 1, :]
    o_ref[...] = (acc_ref[0:dv, :] * (1.0 / l) * gt_ref[...]).astype(BF16)


def _flash(qt, k, vt, gt, *, B, S, heads, dk, dv, kv_of, q_half_of=None, name):
    tq, tk = TQ_FLASH, TK_FLASH
    nq = S // tq
    dkk = k.shape[2]
    n_chunks = S // tk
    assert S % tk == 0 and FLASH_UNROLL % 2 == 0
    kern = functools.partial(_flash_kernel, n_chunks=n_chunks, tk=tk, unroll=FLASH_UNROLL, q_half_of=q_half_of)
    k_of, v_of = kv_of
    return pl.pallas_call(
        kern,
        grid=(B, heads, nq),
        in_specs=[
            pl.BlockSpec((dk, tq), lambda b, g, i: (g, b * nq + i)),
            pl.BlockSpec((1, S, dkk), lambda b, g, i: (k_of(g), b, 0)),
            pl.BlockSpec((dv, S), lambda b, g, i: (v_of(g), b)),
            pl.BlockSpec((dv, tq), lambda b, g, i: (g, b * nq + i)),
        ],
        out_specs=pl.BlockSpec((dv, tq), lambda b, g, i: (g, b * nq + i)),
        out_shape=jax.ShapeDtypeStruct((heads * dv, B * S), BF16),
        scratch_shapes=[pltpu.VMEM((2, tk, tq), F32), pltpu.VMEM((dv + SUM_ROWS, tq), F32)],
        compiler_params=_params(3),
        name=name,
    )(qt, k, vt, gt)


def _flash_mla(qt, k, vt, gt, *, B, S):
    return _flash(qt, k, vt, gt, B=B, S=S, heads=MLA_HEADS, dk=MLA_QK_PAD, dv=MLA_V,
                  kv_of=(lambda g: g, lambda g: g), name="flash_mla")


def _flash_gqa(qt, k, vt, gt, *, B, S):
    return _flash(qt, k, vt, gt, B=B, S=S, heads=GQA_HEADS, dk=GQA_HEAD_DIM, dv=GQA_HEAD_DIM,
                  kv_of=(lambda g: g // (2 * GQA_GROUP), lambda g: g // GQA_GROUP),
                  q_half_of=lambda g: (g // GQA_GROUP) % 2, name="flash_gqa")


def _na_kernel(qt_ref, k_ref, vt_ref, bias_ref, gt_ref, o_ref, *, rows, n_inner):
    hd = pl.program_id(1)
    i = pl.program_id(2)
    half = hd % 2
    half_rows = rows // NA_ROWS_PER_STEP

    def body(t, _):
        j = i * n_inner + t
        qs = pl.multiple_of(t * NA_QT, NA_QT)
        q = qt_ref[:, pl.ds(qs, NA_QT)]
        z = jnp.zeros_like(q)
        q2 = jnp.concatenate([jnp.where(half == 0, q, z), jnp.where(half == 1, q, z)], axis=0)
        kb = jnp.clip(NA_ROWS_PER_STEP * j - NA_WIN_H // 2, 0, rows - NA_KEY_ROWS)
        ks = pl.multiple_of(kb * GRID_W, 2 * GRID_W)
        cls = jnp.where(j < 2, j, jnp.where(j > half_rows - 3, j - (half_rows - NA_CLASSES), 2))
        s = _dot(k_ref[0, pl.ds(ks, NA_KEYS), :], q2) + bias_ref[0, cls]
        m = jnp.max(s, axis=0, keepdims=True)
        p = jnp.exp(s - m)
        l = jnp.sum(p, axis=0, keepdims=True)
        y = _dot(vt_ref[:, pl.ds(ks, NA_KEYS)], p.astype(BF16))
        o_ref[:, pl.ds(qs, NA_QT)] = (y * (1.0 / l) * gt_ref[:, pl.ds(qs, NA_QT)]).astype(BF16)
        return 0

    lax.fori_loop(0, n_inner, body, 0)


def _na_attn(qt, k, vt, bias, gt, *, B, S):
    rows = S // GRID_W
    tq = min(NA_TQ, S)
    nq = S // tq
    n_inner = tq // NA_QT
    kern = functools.partial(_na_kernel, rows=rows, n_inner=n_inner)
    D = NA_HEAD_DIM
    return pl.pallas_call(
        kern,
        grid=(B, NA_HEADS, nq),
        in_specs=[
            pl.BlockSpec((D, tq), lambda b, h, i: (h, b * nq + i)),
            pl.BlockSpec((1, S, 2 * D), lambda b, h, i: (h // 2, b, 0)),
            pl.BlockSpec((D, S), lambda b, h, i: (h, b)),
            pl.BlockSpec((1, NA_CLASSES, NA_KEYS, NA_QT), lambda b, h, i: (h, 0, 0, 0)),
            pl.BlockSpec((D, tq), lambda b, h, i: (h, b * nq + i)),
        ],
        out_specs=pl.BlockSpec((D, tq), lambda b, h, i: (h, b * nq + i)),
        out_shape=jax.ShapeDtypeStruct((NA_WIDTH, B * S), BF16),
        compiler_params=_params(3),
        name="na_attn",
    )(qt, k, vt, bias, gt)


def _na_bias_table(rpb, rows=2 * NA_KEY_ROWS):
    hr = rows // NA_ROWS_PER_STEP
    reps = [0, 1, 2, hr - 2, hr - 1]
    i = np.arange(NA_KEY_ROWS)[:, None, None, None]
    kc = np.arange(GRID_W)[None, :, None, None]
    rr = np.arange(NA_ROWS_PER_STEP)[None, None, :, None]
    qc = np.arange(GRID_W)[None, None, None, :]
    cs = np.clip(qc - NA_WIN_W // 2, 0, GRID_W - NA_WIN_W)
    col_ok = (kc >= cs) & (kc < cs + NA_WIN_W)
    di_sel, ok_all = [], []
    for j in reps:
        r = NA_ROWS_PER_STEP * j + rr
        rs = np.clip(r - NA_WIN_H // 2, 0, rows - NA_WIN_H)
        kb = np.clip(NA_ROWS_PER_STEP * j - NA_WIN_H // 2, 0, rows - NA_KEY_ROWS)
        kr = kb + i
        row_ok = (kr >= rs) & (kr < rs + NA_WIN_H)
        di = np.clip(kr - r + (NA_WIN_H - 1), 0, 2 * NA_WIN_H - 2)
        di_sel.extend(int(v) for v in di.reshape(-1))
        ok_all.append(np.broadcast_to(row_ok & col_ok, (NA_KEY_ROWS, GRID_W, NA_ROWS_PER_STEP, GRID_W)))
    ok = np.stack(ok_all).reshape(NA_CLASSES, NA_KEYS, NA_QT)
    H, nd, nw = rpb.shape
    lo = (GRID_W - 1) - (NA_WIN_W - 1)
    r128 = jnp.pad(rpb, ((0, 0), (0, 0), (lo, 128 - nw - lo)))
    y = jnp.tile(r128, (1, 1, GRID_W))[:, :, :GRID_W * 127].reshape(H, nd, GRID_W, 127)
    toep = y[:, :, :, GRID_W - 1:2 * GRID_W - 1]
    sel = jnp.stack([toep[:, d] for d in di_sel], axis=1)
    sel = sel.reshape(H, NA_CLASSES, NA_KEY_ROWS, NA_ROWS_PER_STEP, GRID_W, GRID_W)
    vals = sel.transpose(0, 1, 2, 5, 3, 4).reshape(H, NA_CLASSES, NA_KEYS, NA_QT)
    return jnp.where(ok[None], vals, NEG_BIG).astype(F32)


def _out_kernel(zna_ref, zmla_ref, zgqa_ref, x_ref, p_ref, wna_ref, wmla_ref, wgqa_ref,
                pn_ref, wpg_ref, wpe_ref, fn_ref, o_ref, *, final):
    mixed_t = (_dot(wna_ref[...], zna_ref[...]) + _dot(wmla_ref[...], zmla_ref[...])
               + _dot(wgqa_ref[...], zgqa_ref[...]))
    x1 = x_ref[...] + mixed_t.T
    hn = _rms_rows(x1, pn_ref[...]).astype(BF16)
    gate = 1.0 / (1.0 + jnp.exp(-_dot(hn, wpg_ref[...])))
    pe = _dot(p_ref[...].astype(BF16), wpe_ref[...])
    x2 = x1 + pe * gate
    if final:
        x2 = _rms_rows(x2, fn_ref[...])
    o_ref[...] = x2


def _out_ple(zna, zmla, zgqa, x, p, wo_na, wo_mla, wo_gqa, ple_norm, wpg, wpe, final_norm, *, final):
    T = x.shape[0]
    tm = TM_OUT
    return pl.pallas_call(
        functools.partial(_out_kernel, final=final),
        grid=(T // tm,),
        in_specs=[
            pl.BlockSpec((NA_WIDTH, tm), lambda i: (0, i)),
            pl.BlockSpec((MLA_WIDTH, tm), lambda i: (0, i)),
            pl.BlockSpec((GQA_WIDTH, tm), lambda i: (0, i)),
            pl.BlockSpec((tm, D_MODEL), lambda i: (i, 0)),
            pl.BlockSpec((tm, PLE_DIM), lambda i: (i, 0)),
            _const_spec((D_MODEL, NA_WIDTH)),
            _const_spec((D_MODEL, MLA_WIDTH)),
            _const_spec((D_MODEL, GQA_WIDTH)),
            _const_spec((1, D_MODEL)),
            _const_spec((D_MODEL, D_MODEL)),
            _const_spec((PLE_DIM, D_MODEL)),
            _const_spec((1, D_MODEL)),
        ],
        out_specs=pl.BlockSpec((tm, D_MODEL), lambda i: (i, 0)),
        out_shape=jax.ShapeDtypeStruct((T, D_MODEL), F32),
        compiler_params=_params(1),
        name="out_ple",
    )(zna, zmla, zgqa, x, p, wo_na, wo_mla, wo_gqa, ple_norm, wpg, wpe, final_norm)


def _rope_tables_t(pos, dim):
    inv = ROPE_THETA ** (-jnp.arange(0, dim, 2, dtype=F32) / dim)
    ang = pos.astype(F32)[:, None] * inv[None, :]
    ang = jnp.concatenate([ang, ang], axis=-1)
    return jnp.cos(ang).T, jnp.sin(ang).T


def _tables(S):
    t = jnp.arange(S)
    cos_t, sin_t = _rope_tables_t(t, MLA_ROPE)
    cos_r, sin_r = _rope_tables_t(t // GRID_W, GQA_HEAD_DIM // 2)
    cos_c, sin_c = _rope_tables_t(t % GRID_W, GQA_HEAD_DIM // 2)
    return (jnp.concatenate([cos_t, sin_t], axis=0),
            jnp.concatenate([cos_r, cos_c, sin_r, sin_c], axis=0))


def _layer_params(i, norm_g, w_in, na_rpb, mla_q_a_norm, mla_w_q_b, mla_kv_a_norm, mla_w_kv_b,
                  gqa_q_norm, gqa_k_norm, w_out, ple_norm, w_pe, w_pg):
    wt = w_in[i].T
    o = 0
    w_na = wt[o:o + 4 * NA_WIDTH]
    w_na = jnp.concatenate([w_na[:NA_WIDTH] * (NA_HEAD_DIM ** -0.5), w_na[NA_WIDTH:]], axis=0)
    o += 4 * NA_WIDTH
    n_mla = MLA_Q_LORA + MLA_KV_LORA + MLA_ROPE + MLA_WIDTH
    w_mla = wt[o:o + n_mla]
    o += n_mla
    w_gqa = wt[o:]
    wo_t = w_out[i].T
    return dict(
        norm_g=norm_g[i][None, :],
        w_na=w_na.astype(BF16), w_mla=w_mla.astype(BF16), w_gqa=w_gqa.astype(BF16),
        qan=mla_q_a_norm[i][:, None], wqb_t=mla_w_q_b[i].T.astype(BF16),
        kvan=mla_kv_a_norm[i][:, None], wkvb_t=mla_w_kv_b[i].T.astype(BF16),
        qn=gqa_q_norm[i][:, None], kn=gqa_k_norm[i][:, None],
        bias=_na_bias_table(na_rpb[i]),
        wo_na=wo_t[:, :NA_WIDTH].astype(BF16),
        wo_mla=wo_t[:, NA_WIDTH:NA_WIDTH + MLA_WIDTH].astype(BF16),
        wo_gqa=wo_t[:, NA_WIDTH + MLA_WIDTH:].astype(BF16),
        ple_norm=ple_norm[i][None, :], wpg=w_pg[i].astype(BF16), wpe=w_pe[i].astype(BF16),
    )


def _run_trunk(x, p, layers, final_norm):
    B, S, _ = x.shape
    T = B * S
    rows = S // GRID_W
    assert S % TM_PROJ == 0 and S % TQ_FLASH == 0 and S % GRID_W == 0
    assert rows >= NA_KEY_ROWS and rows % NA_ROWS_PER_STEP == 0 and rows // NA_ROWS_PER_STEP >= NA_CLASSES
    tab_mla, tab_gqa = _tables(S)
    xf = x.reshape(T, D_MODEL)
    fn = final_norm[None, :]
    depth = len(layers)
    for i, L in enumerate(layers):
        pf = p[i].reshape(T, PLE_DIM)
        na_qt, na_k, na_vt, na_gt = _proj_na(xf, L["norm_g"], L["w_na"])
        ml_qt, ml_k, ml_vt, ml_gt = _proj_mla(xf, L["norm_g"], L["w_mla"], L["qan"], L["wqb_t"],
                                              L["kvan"], L["wkvb_t"], tab_mla, S)
        gq_qt, gq_k, gq_vt, gq_gt = _proj_gqa(xf, L["norm_g"], L["w_gqa"], L["qn"], L["kn"], tab_gqa, S)
        z_na = _na_attn(na_qt, na_k, na_vt, L["bias"], na_gt, B=B, S=S)
        z_mla = _flash_mla(ml_qt, ml_k, ml_vt, ml_gt, B=B, S=S)
        z_gqa = _flash_gqa(gq_qt, gq_k, gq_vt, gq_gt, B=B, S=S)
        xf = _out_ple(z_na, z_mla, z_gqa, xf, pf, L["wo_na"], L["wo_mla"], L["wo_gqa"], L["ple_norm"],
                      L["wpg"], L["wpe"], fn, final=(i == depth - 1))
    return xf.reshape(B, S, D_MODEL)


def kernel(x_prompt, x_sample, p_prompt, p_sample, norm_g, w_in, na_rpb, mla_q_a_norm, mla_w_q_b,
           mla_kv_a_norm, mla_w_kv_b, gqa_q_norm, gqa_k_norm, w_out, ple_norm, w_pe, w_pg, final_norm):
    depth = w_in.shape[0]
    layers = [_layer_params(i, norm_g, w_in, na_rpb, mla_q_a_norm, mla_w_q_b, mla_kv_a_norm, mla_w_kv_b,
                            gqa_q_norm, gqa_k_norm, w_out, ple_norm, w_pe, w_pg) for i in range(depth)]
    y_prompt = _run_trunk(x_prompt, p_prompt, layers, final_norm)
    y_sample = _run_trunk(x_sample, p_sample, layers, final_norm)
    return (y_prompt, y_sample)
```

```python
import functools

import numpy as np
import jax
import jax.numpy as jnp
from jax import lax
from jax.experimental import pallas as pl
from jax.experimental.pallas import tpu as pltpu

F32 = jnp.float32
BF16 = jnp.bfloat16

D_MODEL = 2048
GRID_W = 64
PLE_DIM = 256
EPS = 1e-6
ROPE_THETA = 10000.0

NA_HEADS, NA_HEAD_DIM, NA_WIN_H, NA_WIN_W = 8, 64, 8, 16
NA_WIDTH = NA_HEADS * NA_HEAD_DIM
MLA_HEADS, MLA_NOPE, MLA_ROPE, MLA_V = 6, 128, 64, 128
MLA_Q_LORA, MLA_KV_LORA = 512, 256
MLA_WIDTH = MLA_HEADS * MLA_V
MLA_QK = MLA_NOPE + MLA_ROPE
MLA_QK_PAD = 256
GQA_HEADS, GQA_KV_HEADS, GQA_HEAD_DIM = 12, 4, 64
GQA_GROUP = GQA_HEADS // GQA_KV_HEADS
GQA_WIDTH = GQA_HEADS * GQA_HEAD_DIM
GQA_KV_WIDTH = GQA_KV_HEADS * GQA_HEAD_DIM

VMEM_LIMIT_BYTES = 56 * 1024 * 1024

TM_PROJ = 512
TM_OUT = 512
OUT_ROW_CHUNK = 512
TQ_FLASH = 1024
TK_FLASH = 512
FLASH_UNROLL = 4
SUM_ROWS = 16
NA_ROWS_PER_STEP = 2
NA_KEY_ROWS = 10
NA_KEYS = NA_KEY_ROWS * GRID_W
NA_QT = NA_ROWS_PER_STEP * GRID_W
NA_CLASSES = 5
NA_TQ = 1024
NEG_BIG = -1e30
LOG2E = 1.4426950408889634

_NT = (((1,), (1,)), ((), ()))


def _dot(a, b):
    return jnp.dot(a, b, preferred_element_type=F32)


def _dot_nt(a, b):
    return lax.dot_general(a, b, _NT, preferred_element_type=F32)


def _params(n_axes):
    return pltpu.CompilerParams(
        dimension_semantics=("arbitrary",) * n_axes,
        vmem_limit_bytes=VMEM_LIMIT_BYTES,
    )


def _const_spec(shape):
    nd = len(shape)
    return pl.BlockSpec(shape, lambda *_: (0,) * nd, pipeline_mode=pl.Buffered(1))


def _silu(x):
    return x * (1.0 / (1.0 + jnp.exp(-x)))


def _rms_rows(x, g_row):
    ms = jnp.mean(x * x, axis=-1, keepdims=True)
    return x * lax.rsqrt(ms + EPS) * g_row


def _rms_cols(xt, g_col):
    ms = jnp.mean(xt * xt, axis=0, keepdims=True)
    return xt * lax.rsqrt(ms + EPS) * g_col


def _normed_matmul(x_ref, g_ref, x_fm):
    if x_fm:
        ht = _rms_cols(x_ref[...], g_ref[...]).astype(BF16)
        return lambda w: _dot(w, ht)
    h = _rms_rows(x_ref[...], g_ref[...]).astype(BF16)
    return lambda w: _dot_nt(w, h)


def _x_specs(tm, x_fm):
    if x_fm:
        return [pl.BlockSpec((D_MODEL, tm), lambda i: (0, i)), _const_spec((D_MODEL, 1))]
    return [pl.BlockSpec((tm, D_MODEL), lambda i: (i, 0)), _const_spec((1, D_MODEL))]


def _rot_half_cols(xt, n):
    return jnp.concatenate([-xt[n:2 * n], xt[0:n]], axis=0)


def _proj_na_kernel(x_ref, g_ref, w_ref, qt_ref, k_ref, vt_ref, gt_ref, *, x_fm):
    mm = _normed_matmul(x_ref, g_ref, x_fm)
    W = NA_WIDTH
    qt_ref[...] = mm(w_ref[0:W, :]).astype(BF16)
    kt = mm(w_ref[W:2 * W, :])
    for p in range(NA_HEADS // 2):
        k_ref[p] = kt[128 * p:128 * (p + 1), :].T.astype(BF16)
    vt_ref[...] = mm(w_ref[2 * W:3 * W, :]).astype(BF16)
    gt_ref[...] = _silu(mm(w_ref[3 * W:4 * W, :]))


def _proj_na(x, norm_g, wt, *, x_fm):
    T = x.shape[1 if x_fm else 0]
    tm = TM_PROJ
    return pl.pallas_call(
        functools.partial(_proj_na_kernel, x_fm=x_fm),
        grid=(T // tm,),
        in_specs=_x_specs(tm, x_fm) + [
            _const_spec((4 * NA_WIDTH, D_MODEL)),
        ],
        out_specs=[
            pl.BlockSpec((NA_WIDTH, tm), lambda i: (0, i)),
            pl.BlockSpec((NA_HEADS // 2, tm, 128), lambda i: (0, i, 0)),
            pl.BlockSpec((NA_WIDTH, tm), lambda i: (0, i)),
            pl.BlockSpec((NA_WIDTH, tm), lambda i: (0, i)),
        ],
        out_shape=[
            jax.ShapeDtypeStruct((NA_WIDTH, T), BF16),
            jax.ShapeDtypeStruct((NA_HEADS // 2, T, 128), BF16),
            jax.ShapeDtypeStruct((NA_WIDTH, T), BF16),
            jax.ShapeDtypeStruct((NA_WIDTH, T), F32),
        ],
        compiler_params=_params(1),
        name="proj_na",
    )(x, norm_g, wt)


def _proj_gqa_kernel(x_ref, g_ref, w_ref, qn_ref, kn_ref, tab_ref, qt_ref, k_ref, vt_ref, gt_ref, *, x_fm):
    mm = _normed_matmul(x_ref, g_ref, x_fm)
    cos = tab_ref[0:GQA_HEAD_DIM, :]
    sin = tab_ref[GQA_HEAD_DIM:2 * GQA_HEAD_DIM, :]
    q4 = GQA_HEAD_DIM // 4

    def norm_rope(xt, g_col):
        y = _rms_cols(xt, g_col)
        rot = jnp.concatenate([-y[q4:2 * q4], y[0:q4], -y[3 * q4:4 * q4], y[2 * q4:3 * q4]], axis=0)
        return y * cos + rot * sin

    o = 0
    qt = mm(w_ref[o:o + GQA_WIDTH, :])
    for hd in range(GQA_HEADS):
        r = slice(GQA_HEAD_DIM * hd, GQA_HEAD_DIM * (hd + 1))
        qt_ref[r, :] = (norm_rope(qt[r, :], qn_ref[...]) * (GQA_HEAD_DIM ** -0.5 * LOG2E)).astype(BF16)
    o += GQA_WIDTH
    kt = mm(w_ref[o:o + GQA_KV_WIDTH, :])
    for p in range(GQA_KV_HEADS // 2):
        pair = jnp.concatenate(
            [norm_rope(kt[GQA_HEAD_DIM * (2 * p + e):GQA_HEAD_DIM * (2 * p + e + 1), :], kn_ref[...])
             for e in range(2)], axis=0)
        k_ref[p] = pair.T.astype(BF16)
    o += GQA_KV_WIDTH
    vt_ref[...] = mm(w_ref[o:o + GQA_KV_WIDTH, :]).astype(BF16)
    o += GQA_KV_WIDTH
    gt_ref[...] = _silu(mm(w_ref[o:o + GQA_WIDTH, :]))


def _proj_gqa(x, norm_g, wt, qn_col, kn_col, tabs, S, *, x_fm):
    T = x.shape[1 if x_fm else 0]
    tm = TM_PROJ
    nb = S // tm
    rows = 2 * GQA_WIDTH + 2 * GQA_KV_WIDTH
    return pl.pallas_call(
        functools.partial(_proj_gqa_kernel, x_fm=x_fm),
        grid=(T // tm,),
        in_specs=_x_specs(tm, x_fm) + [
            _const_spec((rows, D_MODEL)),
            _const_spec((GQA_HEAD_DIM, 1)),
            _const_spec((GQA_HEAD_DIM, 1)),
            pl.BlockSpec((2 * GQA_HEAD_DIM, tm), lambda i: (0, i % nb)),
        ],
        out_specs=[
            pl.BlockSpec((GQA_WIDTH, tm), lambda i: (0, i)),
            pl.BlockSpec((GQA_KV_HEADS // 2, tm, 128), lambda i: (0, i, 0)),
            pl.BlockSpec((GQA_KV_WIDTH, tm), lambda i: (0, i)),
            pl.BlockSpec((GQA_WIDTH, tm), lambda i: (0, i)),
        ],
        out_shape=[
            jax.ShapeDtypeStruct((GQA_WIDTH, T), BF16),
            jax.ShapeDtypeStruct((GQA_KV_HEADS // 2, T, 128), BF16),
            jax.ShapeDtypeStruct((GQA_KV_WIDTH, T), BF16),
            jax.ShapeDtypeStruct((GQA_WIDTH, T), F32),
        ],
        compiler_params=_params(1),
        name="proj_gqa",
    )(x, norm_g, wt, qn_col, kn_col, tabs)


def _proj_mla_kernel(x_ref, g_ref, w_ref, qan_ref, wqb_ref, kvan_ref, wkvb_ref, tab_ref,
                     qt_ref, k_ref, vt_ref, gt_ref, *, x_fm):
    mm = _normed_matmul(x_ref, g_ref, x_fm)
    cos = tab_ref[0:MLA_ROPE, :]
    sin = tab_ref[MLA_ROPE:2 * MLA_ROPE, :]
    tm = tab_ref.shape[1]
    half = MLA_ROPE // 2
    scale = MLA_QK ** -0.5 * LOG2E

    def rope(xt):
        return xt * cos + _rot_half_cols(xt, half) * sin

    o = 0
    cq = _rms_cols(mm(w_ref[o:o + MLA_Q_LORA, :]), qan_ref[...]).astype(BF16)
    qt = _dot(wqb_ref[...], cq)
    zpad = jnp.zeros((MLA_QK_PAD - MLA_QK, tm), BF16)
    for hd in range(MLA_HEADS):
        b = MLA_QK * hd
        ob = MLA_QK_PAD * hd
        qt_ref[ob:ob + MLA_NOPE, :] = (qt[b:b + MLA_NOPE, :] * scale).astype(BF16)
        qt_ref[ob + MLA_NOPE:ob + MLA_QK, :] = (rope(qt[b + MLA_NOPE:b + MLA_QK, :]) * scale).astype(BF16)
        qt_ref[ob + MLA_QK:ob + MLA_QK_PAD, :] = zpad
    o += MLA_Q_LORA
    ckv = _rms_cols(mm(w_ref[o:o + MLA_KV_LORA, :]), kvan_ref[...]).astype(BF16)
    kvt = _dot(wkvb_ref[...], ckv)
    o += MLA_KV_LORA
    krt = rope(mm(w_ref[o:o + MLA_ROPE, :]))
    kr = jnp.concatenate([krt, jnp.zeros_like(krt)], axis=0).T.astype(BF16)
    o += MLA_ROPE
    for hd in range(MLA_HEADS):
        b = (MLA_NOPE + MLA_V) * hd
        k_ref[hd, :, 0:MLA_NOPE] = kvt[b:b + MLA_NOPE, :].T.astype(BF16)
        k_ref[hd, :, MLA_NOPE:MLA_QK_PAD] = kr
        vt_ref[MLA_V * hd:MLA_V * (hd + 1), :] = kvt[b + MLA_NOPE:b + MLA_NOPE + MLA_V, :].astype(BF16)
    gt_ref[...] = _silu(mm(w_ref[o:o + MLA_WIDTH, :]))


def _proj_mla(x, norm_g, wt, qan_col, wqb_t, kvan_col, wkvb_t, tabs, S, *, x_fm):
    T = x.shape[1 if x_fm else 0]
    tm = TM_PROJ
    nb = S // tm
    rows = MLA_Q_LORA + MLA_KV_LORA + MLA_ROPE + MLA_WIDTH
    return pl.pallas_call(
        functools.partial(_proj_mla_kernel, x_fm=x_fm),
        grid=(T // tm,),
        in_specs=_x_specs(tm, x_fm) + [
            _const_spec((rows, D_MODEL)),
            _const_spec((MLA_Q_LORA, 1)),
            _const_spec((MLA_HEADS * MLA_QK, MLA_Q_LORA)),
            _const_spec((MLA_KV_LORA, 1)),
            _const_spec((MLA_HEADS * (MLA_NOPE + MLA_V), MLA_KV_LORA)),
            pl.BlockSpec((2 * MLA_ROPE, tm), lambda i: (0, i % nb)),
        ],
        out_specs=[
            pl.BlockSpec((MLA_HEADS * MLA_QK_PAD, tm), lambda i: (0, i)),
            pl.BlockSpec((MLA_HEADS, tm, MLA_QK_PAD), lambda i: (0, i, 0)),
            pl.BlockSpec((MLA_WIDTH, tm), lambda i: (0, i)),
            pl.BlockSpec((MLA_WIDTH, tm), lambda i: (0, i)),
        ],
        out_shape=[
            jax.ShapeDtypeStruct((MLA_HEADS * MLA_QK_PAD, T), BF16),
            jax.ShapeDtypeStruct((MLA_HEADS, T, MLA_QK_PAD), BF16),
            jax.ShapeDtypeStruct((MLA_WIDTH, T), BF16),
            jax.ShapeDtypeStruct((MLA_WIDTH, T), F32),
        ],
        compiler_params=_params(1),
        name="proj_mla",
    )(x, norm_g, wt, qan_col, wqb_t, kvan_col, wkvb_t, tabs)


def _flash_kernel(qt_ref, k_ref, vt_ref, gt_ref, o_ref, s_ref, acc_ref, *, n_chunks, tk, unroll, q_half_of):
    g = pl.program_id(1)
    qt = qt_ref[...]
    if q_half_of is not None:
        half = q_half_of(g)
        z = jnp.zeros_like(qt)
        qt = jnp.concatenate([jnp.where(half == 0, qt, z), jnp.where(half == 1, qt, z)], axis=0)
    tq = qt.shape[1]
    dv = vt_ref.shape[0]
    acc_ref[...] = jnp.zeros_like(acc_ref)
    ones = jnp.ones((SUM_ROWS, tk), BF16)

    def stage1(c, slot):
        start = pl.multiple_of(c * tk, tk)
        s = _dot(k_ref[0, pl.ds(start, tk), :], qt)
        s_ref[slot] = s
        return jnp.max(s, axis=0, keepdims=True)

    def stage2(c, slot, m, m_cur):
        start = pl.multiple_of(c * tk, tk)
        m_new = jnp.maximum(m, m_cur)
        alpha = jnp.exp2(m - m_new)
        p = jnp.exp2(s_ref[slot] - m_new).astype(BF16)
        v1 = jnp.concatenate([vt_ref[:, pl.ds(start, tk)], ones], axis=0)
        acc_ref[...] = alpha * acc_ref[...] + _dot(v1, p)
        return m_new

    def group(i, carry):
        m, m_cur = carry
        c = unroll * i
        for u in range(unroll):
            m_nxt = stage1(c + u + 1, (u + 1) % 2)
            m = stage2(c + u, u % 2, m, m_cur)
            m_cur = m_nxt
        return m, m_cur

    n_groups = (n_chunks - 1) // unroll
    m = jnp.full((1, tq), -jnp.inf, F32)
    m_cur = stage1(0, 0)
    m, m_cur = lax.fori_loop(0, n_groups, group, (m, m_cur))
    for c in range(n_groups * unroll, n_chunks):
        m_nxt = stage1(c + 1, (c + 1) % 2) if c + 1 < n_chunks else None
        m = stage2(c, c % 2, m, m_cur)
        m_cur = m_nxt
    l = acc_ref[dv:dv + 1, :]
    o_ref[...] = (acc_ref[0:dv, :] * (1.0 / l) * gt_ref[...]).astype(BF16)


def _flash(qt, k, vt, gt, *, B, S, heads, dk, dv, kv_of, q_half_of=None, name):
    tq, tk = min(TQ_FLASH, S), TK_FLASH
    nq = S // tq
    dkk = k.shape[2]
    n_chunks = S // tk
    assert S % tk == 0 and FLASH_UNROLL % 2 == 0
    kern = functools.partial(_flash_kernel, n_chunks=n_chunks, tk=tk, unroll=FLASH_UNROLL, q_half_of=q_half_of)
    k_of, v_of = kv_of
    return pl.pallas_call(
        kern,
        grid=(B, heads, nq),
        in_specs=[
            pl.BlockSpec((dk, tq), lambda b, g, i: (g, b * nq + i)),
            pl.BlockSpec((1, S, dkk), lambda b, g, i: (k_of(g), b, 0)),
            pl.BlockSpec((dv, S), lambda b, g, i: (v_of(g), b)),
            pl.BlockSpec((dv, tq), lambda b, g, i: (g, b * nq + i)),
        ],
        out_specs=pl.BlockSpec((dv, tq), lambda b, g, i: (g, b * nq + i)),
        out_shape=jax.ShapeDtypeStruct((heads * dv, B * S), BF16),
        scratch_shapes=[pltpu.VMEM((2, tk, tq), F32), pltpu.VMEM((dv + SUM_ROWS, tq), F32)],
        compiler_params=_params(3),
        name=name,
    )(qt, k, vt, gt)


def _flash_mla(qt, k, vt, gt, *, B, S):
    return _flash(qt, k, vt, gt, B=B, S=S, heads=MLA_HEADS, dk=MLA_QK_PAD, dv=MLA_V,
                  kv_of=(lambda g: g, lambda g: g), name="flash_mla")


def _flash_gqa(qt, k, vt, gt, *, B, S):
    return _flash(qt, k, vt, gt, B=B, S=S, heads=GQA_HEADS, dk=GQA_HEAD_DIM, dv=GQA_HEAD_DIM,
                  kv_of=(lambda g: g // (2 * GQA_GROUP), lambda g: g // GQA_GROUP),
                  q_half_of=lambda g: (g // GQA_GROUP) % 2, name="flash_gqa")


def _na_kernel(qt_ref, k_ref, vt_ref, bias_ref, gt_ref, o_ref, *, rows, n_inner):
    i = pl.program_id(2)
    half_rows = rows // NA_ROWS_PER_STEP
    D = NA_HEAD_DIM
    top = lax.broadcasted_iota(jnp.int32, (2 * D, NA_QT), 0) < D
    ones = jnp.ones((SUM_ROWS, NA_KEYS), BF16)

    def scores(t):
        j = i * n_inner + t
        q = qt_ref[:, t * NA_QT:(t + 1) * NA_QT]
        z = jnp.zeros_like(q)
        q2 = jnp.concatenate([jnp.where(top, q, z), jnp.where(top, z, q)], axis=1)
        kb = jnp.clip(NA_ROWS_PER_STEP * j - NA_WIN_H // 2, 0, rows - NA_KEY_ROWS)
        ks = pl.multiple_of(kb * GRID_W, 2 * GRID_W)
        cls = jnp.where(j < 2, j, jnp.where(j > half_rows - 3, j - (half_rows - NA_CLASSES), 2))
        return _dot(k_ref[0, pl.ds(ks, NA_KEYS), :], q2) + bias_ref[0, cls], ks

    def finish(t, s, ks):
        m = jnp.max(s, axis=0, keepdims=True)
        p = jnp.exp(s - m).astype(BF16)
        for e in range(2):
            v1 = jnp.concatenate([vt_ref[D * e:D * (e + 1), pl.ds(ks, NA_KEYS)], ones], axis=0)
            y = _dot(v1, p[:, e * NA_QT:(e + 1) * NA_QT])
            g = gt_ref[D * e:D * (e + 1), t * NA_QT:(t + 1) * NA_QT]
            o_ref[D * e:D * (e + 1), t * NA_QT:(t + 1) * NA_QT] = (y[0:D] * (1.0 / y[D:D + 1]) * g).astype(BF16)

    cur = scores(0)
    for t in range(n_inner):
        nxt = scores(t + 1) if t + 1 < n_inner else None
        finish(t, *cur)
        cur = nxt


def _na_attn(qt, k, vt, bias, gt, *, B, S):
    rows = S // GRID_W
    tq = min(NA_TQ, S)
    nq = S // tq
    n_inner = tq // NA_QT
    kern = functools.partial(_na_kernel, rows=rows, n_inner=n_inner)
    D2 = 2 * NA_HEAD_DIM
    return pl.pallas_call(
        kern,
        grid=(B, NA_HEADS // 2, nq),
        in_specs=[
            pl.BlockSpec((D2, tq), lambda b, h, i: (h, b * nq + i)),
            pl.BlockSpec((1, S, D2), lambda b, h, i: (h, b, 0)),
            pl.BlockSpec((D2, S), lambda b, h, i: (h, b)),
            pl.BlockSpec((1, NA_CLASSES, NA_KEYS, 2 * NA_QT), lambda b, h, i: (h, 0, 0, 0)),
            pl.BlockSpec((D2, tq), lambda b, h, i: (h, b * nq + i)),
        ],
        out_specs=pl.BlockSpec((D2, tq), lambda b, h, i: (h, b * nq + i)),
        out_shape=jax.ShapeDtypeStruct((NA_WIDTH, B * S), BF16),
        compiler_params=_params(3),
        name="na_attn",
    )(qt, k, vt, bias, gt)


def _na_bias_table(rpb, rows=2 * NA_KEY_ROWS):
    hr = rows // NA_ROWS_PER_STEP
    reps = [0, 1, 2, hr - 2, hr - 1]
    i = np.arange(NA_KEY_ROWS)[:, None, None, None]
    kc = np.arange(GRID_W)[None, :, None, None]
    rr = np.arange(NA_ROWS_PER_STEP)[None, None, :, None]
    qc = np.arange(GRID_W)[None, None, None, :]
    cs = np.clip(qc - NA_WIN_W // 2, 0, GRID_W - NA_WIN_W)
    col_ok = (kc >= cs) & (kc < cs + NA_WIN_W)
    di_sel, ok_all = [], []
    for j in reps:
        r = NA_ROWS_PER_STEP * j + rr
        rs = np.clip(r - NA_WIN_H // 2, 0, rows - NA_WIN_H)
        kb = np.clip(NA_ROWS_PER_STEP * j - NA_WIN_H // 2, 0, rows - NA_KEY_ROWS)
        kr = kb + i
        row_ok = (kr >= rs) & (kr < rs + NA_WIN_H)
        di = np.clip(kr - r + (NA_WIN_H - 1), 0, 2 * NA_WIN_H - 2)
        di_sel.extend(int(v) for v in di.reshape(-1))
        ok_all.append(np.broadcast_to(row_ok & col_ok, (NA_KEY_ROWS, GRID_W, NA_ROWS_PER_STEP, GRID_W)))
    ok = np.stack(ok_all).reshape(NA_CLASSES, NA_KEYS, NA_QT)
    H, nd, nw = rpb.shape
    lo = (GRID_W - 1) - (NA_WIN_W - 1)
    r128 = jnp.pad(rpb, ((0, 0), (0, 0), (lo, 128 - nw - lo)))
    y = jnp.tile(r128, (1, 1, GRID_W))[:, :, :GRID_W * 127].reshape(H, nd, GRID_W, 127)
    toep = y[:, :, :, GRID_W - 1:2 * GRID_W - 1]
    sel = jnp.stack([toep[:, d] for d in di_sel], axis=1)
    sel = sel.reshape(H, NA_CLASSES, NA_KEY_ROWS, NA_ROWS_PER_STEP, GRID_W, GRID_W)
    vals = sel.transpose(0, 1, 2, 5, 3, 4).reshape(H, NA_CLASSES, NA_KEYS, NA_QT)
    tab = jnp.where(ok[None], vals, NEG_BIG).astype(F32)
    tab = tab.reshape(H // 2, 2, NA_CLASSES, NA_KEYS, NA_QT).transpose(0, 2, 3, 1, 4)
    return tab.reshape(H // 2, NA_CLASSES, NA_KEYS, 2 * NA_QT)


def _out_kernel(zna_ref, zmla_ref, zgqa_ref, x_ref, p_ref, wna_ref, wmla_ref, wgqa_ref,
                pn_ref, wpg_ref, wpe_ref, fn_ref, o_ref, x1_ref, hn_ref, *, x_fm, final):
    tm = x1_ref.shape[1]
    chunks = [slice(r, r + OUT_ROW_CHUNK) for r in range(0, D_MODEL, OUT_ROW_CHUNK)]
    xt = x_ref[...] if x_fm else x_ref[...].T
    ss = jnp.zeros((1, tm), F32)
    for r in chunks:
        x1 = xt[r, :] + (_dot(wna_ref[r, :], zna_ref[...]) + _dot(wmla_ref[r, :], zmla_ref[...])
                         + _dot(wgqa_ref[r, :], zgqa_ref[...]))
        x1_ref[r, :] = x1
        ss = ss + jnp.sum(x1 * x1, axis=0, keepdims=True)
    inv = lax.rsqrt(ss * (1.0 / D_MODEL) + EPS)
    for r in chunks:
        hn_ref[r, :] = (x1_ref[r, :] * inv * pn_ref[r, :]).astype(BF16)
    pb = p_ref[...].astype(BF16)
    ss = jnp.zeros((1, tm), F32)
    for r in chunks:
        gate = 1.0 / (1.0 + jnp.exp(-_dot(wpg_ref[r, :], hn_ref[...])))
        x2 = x1_ref[r, :] + _dot_nt(wpe_ref[r, :], pb) * gate
        if final:
            x1_ref[r, :] = x2
            ss = ss + jnp.sum(x2 * x2, axis=0, keepdims=True)
        else:
            o_ref[r, :] = x2
    if final:
        inv = lax.rsqrt(ss * (1.0 / D_MODEL) + EPS)
        for r in chunks:
            o_ref[:, r] = (x1_ref[r, :] * inv * fn_ref[r, :]).T


def _out_ple(zna, zmla, zgqa, x, p, wo_na, wo_mla, wo_gqa, pn_col, wpg_t, wpe_t, fn_col, *, x_fm, final):
    T = p.shape[0]
    tm = TM_OUT
    x_spec = pl.BlockSpec((D_MODEL, tm), lambda i: (0, i)) if x_fm else pl.BlockSpec((tm, D_MODEL), lambda i: (i, 0))
    if final:
        out_spec, out_shape = pl.BlockSpec((tm, D_MODEL), lambda i: (i, 0)), (T, D_MODEL)
    else:
        out_spec, out_shape = pl.BlockSpec((D_MODEL, tm), lambda i: (0, i)), (D_MODEL, T)
    return pl.pallas_call(
        functools.partial(_out_kernel, x_fm=x_fm, final=final),
        grid=(T // tm,),
        in_specs=[
            pl.BlockSpec((NA_WIDTH, tm), lambda i: (0, i)),
            pl.BlockSpec((MLA_WIDTH, tm), lambda i: (0, i)),
            pl.BlockSpec((GQA_WIDTH, tm), lambda i: (0, i)),
            x_spec,
            pl.BlockSpec((tm, PLE_DIM), lambda i: (i, 0)),
            _const_spec((D_MODEL, NA_WIDTH)),
            _const_spec((D_MODEL, MLA_WIDTH)),
            _const_spec((D_MODEL, GQA_WIDTH)),
            _const_spec((D_MODEL, 1)),
            _const_spec((D_MODEL, D_MODEL)),
            _const_spec((D_MODEL, PLE_DIM)),
            _const_spec((D_MODEL, 1)),
        ],
        out_specs=out_spec,
        out_shape=jax.ShapeDtypeStruct(out_shape, F32),
        scratch_shapes=[pltpu.VMEM((D_MODEL, tm), F32), pltpu.VMEM((D_MODEL, tm), BF16)],
        compiler_params=_params(1),
        name="out_ple",
    )(zna, zmla, zgqa, x, p, wo_na, wo_mla, wo_gqa, pn_col, wpg_t, wpe_t, fn_col)


def _rope_tables_t(pos, dim):
    inv = ROPE_THETA ** (-jnp.arange(0, dim, 2, dtype=F32) / dim)
    ang = pos.astype(F32)[:, None] * inv[None, :]
    ang = jnp.concatenate([ang, ang], axis=-1)
    return jnp.cos(ang).T, jnp.sin(ang).T


def _tables(S):
    t = jnp.arange(S)
    cos_t, sin_t = _rope_tables_t(t, MLA_ROPE)
    cos_r, sin_r = _rope_tables_t(t // GRID_W, GQA_HEAD_DIM // 2)
    cos_c, sin_c = _rope_tables_t(t % GRID_W, GQA_HEAD_DIM // 2)
    return (jnp.concatenate([cos_t, sin_t], axis=0),
            jnp.concatenate([cos_r, cos_c, sin_r, sin_c], axis=0))


def _layer_params(i, norm_g, w_in, na_rpb, mla_q_a_norm, mla_w_q_b, mla_kv_a_norm, mla_w_kv_b,
                  gqa_q_norm, gqa_k_norm, w_out, ple_norm, w_pe, w_pg):
    wt = w_in[i].T
    o = 0
    w_na = wt[o:o + 4 * NA_WIDTH]
    w_na = jnp.concatenate([w_na[:NA_WIDTH] * (NA_HEAD_DIM ** -0.5), w_na[NA_WIDTH:]], axis=0)
    o += 4 * NA_WIDTH
    n_mla = MLA_Q_LORA + MLA_KV_LORA + MLA_ROPE + MLA_WIDTH
    w_mla = wt[o:o + n_mla]
    o += n_mla
    w_gqa = wt[o:]
    wo_t = w_out[i].T
    return dict(
        norm_g=norm_g[i][None, :], norm_g_col=norm_g[i][:, None],
        w_na=w_na.astype(BF16), w_mla=w_mla.astype(BF16), w_gqa=w_gqa.astype(BF16),
        qan=mla_q_a_norm[i][:, None], wqb_t=mla_w_q_b[i].T.astype(BF16),
        kvan=mla_kv_a_norm[i][:, None], wkvb_t=mla_w_kv_b[i].T.astype(BF16),
        qn=gqa_q_norm[i][:, None], kn=gqa_k_norm[i][:, None],
        bias=_na_bias_table(na_rpb[i]),
        wo_na=wo_t[:, :NA_WIDTH].astype(BF16),
        wo_mla=wo_t[:, NA_WIDTH:NA_WIDTH + MLA_WIDTH].astype(BF16),
        wo_gqa=wo_t[:, NA_WIDTH + MLA_WIDTH:].astype(BF16),
        ple_norm=ple_norm[i][:, None], wpg_t=w_pg[i].T.astype(BF16), wpe_t=w_pe[i].T.astype(BF16),
    )


def _run_trunk(x, p, layers, final_norm):
    B, S, _ = x.shape
    T = B * S
    rows = S // GRID_W
    assert S % TM_PROJ == 0 and S % TM_OUT == 0 and S % GRID_W == 0
    assert rows >= NA_KEY_ROWS and rows % NA_ROWS_PER_STEP == 0 and rows // NA_ROWS_PER_STEP >= NA_CLASSES
    tab_mla, tab_gqa = _tables(S)
    xf = x.reshape(T, D_MODEL)
    fn = final_norm[:, None]
    depth = len(layers)
    for i, L in enumerate(layers):
        x_fm = i > 0
        g = L["norm_g_col"] if x_fm else L["norm_g"]
        pf = p[i].reshape(T, PLE_DIM)
        na_qt, na_k, na_vt, na_gt = _proj_na(xf, g, L["w_na"], x_fm=x_fm)
        ml_qt, ml_k, ml_vt, ml_gt = _proj_mla(xf, g, L["w_mla"], L["qan"], L["wqb_t"],
                                              L["kvan"], L["wkvb_t"], tab_mla, S, x_fm=x_fm)
        gq_qt, gq_k, gq_vt, gq_gt = _proj_gqa(xf, g, L["w_gqa"], L["qn"], L["kn"], tab_gqa, S, x_fm=x_fm)
        z_na = _na_attn(na_qt, na_k, na_vt, L["bias"], na_gt, B=B, S=S)
        z_mla = _flash_mla(ml_qt, ml_k, ml_vt, ml_gt, B=B, S=S)
        z_gqa = _flash_gqa(gq_qt, gq_k, gq_vt, gq_gt, B=B, S=S)
        xf = _out_ple(z_na, z_mla, z_gqa, xf, pf, L["wo_na"], L["wo_mla"], L["wo_gqa"], L["ple_norm"],
                      L["wpg_t"], L["wpe_t"], fn, x_fm=x_fm, final=(i == depth - 1))
    return xf.reshape(B, S, D_MODEL)


def kernel(x_prompt, x_sample, p_prompt, p_sample, norm_g, w_in, na_rpb, mla_q_a_norm, mla_w_q_b,
           mla_kv_a_norm, mla_w_kv_b, gqa_q_norm, gqa_k_norm, w_out, ple_norm, w_pe, w_pg, final_norm):
    depth = w_in.shape[0]
    layers = [_layer_params(i, norm_g, w_in, na_rpb, mla_q_a_norm, mla_w_q_b, mla_kv_a_norm, mla_w_kv_b,
                            gqa_q_norm, gqa_k_norm, w_out, ple_norm, w_pe, w_pg) for i in range(depth)]
    y_prompt = _run_trunk(x_prompt, p_prompt, layers, final_norm)
    y_sample = _run_trunk(x_sample, p_sample, layers, final_norm)
    return (y_prompt, y_sample)
```

```python
import functools
import math

import numpy as np
import jax
import jax.numpy as jnp
from jax import lax
from jax.experimental import pallas as pl
from jax.experimental.pallas import tpu as pltpu

F32 = jnp.float32
BF16 = jnp.bfloat16

D_MODEL = 2048
GRID_W = 64
PLE_DIM = 256
EPS = 1e-6
ROPE_THETA = 10000.0

NA_HEADS, NA_HEAD_DIM, NA_WIN_H, NA_WIN_W = 8, 64, 8, 16
NA_WIDTH = NA_HEADS * NA_HEAD_DIM
MLA_HEADS, MLA_NOPE, MLA_ROPE, MLA_V = 6, 128, 64, 128
MLA_Q_LORA, MLA_KV_LORA = 512, 256
MLA_WIDTH = MLA_HEADS * MLA_V
MLA_QK = MLA_NOPE + MLA_ROPE
MLA_QK_PAD = 256
GQA_HEADS, GQA_KV_HEADS, GQA_HEAD_DIM = 12, 4, 64
GQA_GROUP = GQA_HEADS // GQA_KV_HEADS
GQA_WIDTH = GQA_HEADS * GQA_HEAD_DIM
GQA_KV_WIDTH = GQA_KV_HEADS * GQA_HEAD_DIM

VMEM_LIMIT_BYTES = 56 * 1024 * 1024

TM_PROJ = 1024
TM_OUT = 512
OUT_ROW_CHUNK = 512
SUM_ROWS = 16
FLASH_TQ_LONG = 1024
FLASH_TQ_SHORT = 2048
FLASH_KEY_TILES = {GQA_HEAD_DIM: (256, 16), MLA_V: (512, 4)}
NA_ROWS_PER_STEP = 2
NA_KEY_ROWS = 10
NA_KEYS = NA_KEY_ROWS * GRID_W
NA_QT = NA_ROWS_PER_STEP * GRID_W
NA_CLASSES = 5
NA_TQ = 2048
NEG_BIG = -1e30
LOG2E = 1.4426950408889634

_NT = (((1,), (1,)), ((), ()))


def _dot(a, b):
    return jnp.dot(a, b, preferred_element_type=F32)


def _dot_nt(a, b):
    return lax.dot_general(a, b, _NT, preferred_element_type=F32)


def _params(n_axes):
    return pltpu.CompilerParams(
        dimension_semantics=("arbitrary",) * n_axes,
        vmem_limit_bytes=VMEM_LIMIT_BYTES,
    )


def _const_spec(shape):
    nd = len(shape)
    return pl.BlockSpec(shape, lambda *_: (0,) * nd, pipeline_mode=pl.Buffered(1))


def _silu(x):
    return x * (1.0 / (1.0 + jnp.exp(-x)))


def _rms_rows(x, g_row):
    ms = jnp.mean(x * x, axis=-1, keepdims=True)
    return x * lax.rsqrt(ms + EPS) * g_row


def _rms_cols(xt, g_col):
    ms = jnp.mean(xt * xt, axis=0, keepdims=True)
    return xt * lax.rsqrt(ms + EPS) * g_col


def _normed_matmul(x_ref, g_ref, x_fm):
    if x_fm:
        ht = _rms_cols(x_ref[...], g_ref[...]).astype(BF16)
        return lambda w: _dot(w, ht)
    h = _rms_rows(x_ref[...], g_ref[...]).astype(BF16)
    return lambda w: _dot_nt(w, h)


def _x_specs(tm, x_fm):
    if x_fm:
        return [pl.BlockSpec((D_MODEL, tm), lambda i: (0, i)), _const_spec((D_MODEL, 1))]
    return [pl.BlockSpec((tm, D_MODEL), lambda i: (i, 0)), _const_spec((1, D_MODEL))]


def _rot_half_cols(xt, n):
    return jnp.concatenate([-xt[n:2 * n], xt[0:n]], axis=0)


def _proj_na_kernel(x_ref, g_ref, w_ref, qt_ref, k_ref, vt_ref, gt_ref, *, x_fm):
    mm = _normed_matmul(x_ref, g_ref, x_fm)
    W = NA_WIDTH
    qt_ref[...] = mm(w_ref[0:W, :]).astype(BF16)
    kt = mm(w_ref[W:2 * W, :])
    for p in range(NA_HEADS // 2):
        k_ref[p] = kt[128 * p:128 * (p + 1), :].T.astype(BF16)
    vt_ref[...] = mm(w_ref[2 * W:3 * W, :]).astype(BF16)
    gt_ref[...] = _silu(mm(w_ref[3 * W:4 * W, :]))


def _proj_na(x, norm_g, wt, *, x_fm):
    T = x.shape[1 if x_fm else 0]
    tm = TM_PROJ
    return pl.pallas_call(
        functools.partial(_proj_na_kernel, x_fm=x_fm),
        grid=(T // tm,),
        in_specs=_x_specs(tm, x_fm) + [
            _const_spec((4 * NA_WIDTH, D_MODEL)),
        ],
        out_specs=[
            pl.BlockSpec((NA_WIDTH, tm), lambda i: (0, i)),
            pl.BlockSpec((NA_HEADS // 2, tm, 128), lambda i: (0, i, 0)),
            pl.BlockSpec((NA_WIDTH, tm), lambda i: (0, i)),
            pl.BlockSpec((NA_WIDTH, tm), lambda i: (0, i)),
        ],
        out_shape=[
            jax.ShapeDtypeStruct((NA_WIDTH, T), BF16),
            jax.ShapeDtypeStruct((NA_HEADS // 2, T, 128), BF16),
            jax.ShapeDtypeStruct((NA_WIDTH, T), BF16),
            jax.ShapeDtypeStruct((NA_WIDTH, T), F32),
        ],
        compiler_params=_params(1),
        name="proj_na",
    )(x, norm_g, wt)


def _proj_gqa_kernel(x_ref, g_ref, w_ref, qn_ref, kn_ref, tab_ref, qt_ref, k_ref, vt_ref, gt_ref, *, x_fm):
    mm = _normed_matmul(x_ref, g_ref, x_fm)
    cos = tab_ref[0:GQA_HEAD_DIM, :]
    sin = tab_ref[GQA_HEAD_DIM:2 * GQA_HEAD_DIM, :]
    q4 = GQA_HEAD_DIM // 4

    def norm_rope(xt, g_col):
        y = _rms_cols(xt, g_col)
        rot = jnp.concatenate([-y[q4:2 * q4], y[0:q4], -y[3 * q4:4 * q4], y[2 * q4:3 * q4]], axis=0)
        return y * cos + rot * sin

    o = 0
    qt = mm(w_ref[o:o + GQA_WIDTH, :])
    for hd in range(GQA_HEADS):
        r = slice(GQA_HEAD_DIM * hd, GQA_HEAD_DIM * (hd + 1))
        qt_ref[r, :] = (norm_rope(qt[r, :], qn_ref[...]) * (GQA_HEAD_DIM ** -0.5 * LOG2E)).astype(BF16)
    o += GQA_WIDTH
    kt = mm(w_ref[o:o + GQA_KV_WIDTH, :])
    for p in range(GQA_KV_HEADS // 2):
        pair = jnp.concatenate(
            [norm_rope(kt[GQA_HEAD_DIM * (2 * p + e):GQA_HEAD_DIM * (2 * p + e + 1), :], kn_ref[...])
             for e in range(2)], axis=0)
        k_ref[p] = pair.T.astype(BF16)
    o += GQA_KV_WIDTH
    vt_ref[...] = mm(w_ref[o:o + GQA_KV_WIDTH, :]).astype(BF16)
    o += GQA_KV_WIDTH
    gt_ref[...] = _silu(mm(w_ref[o:o + GQA_WIDTH, :]))


def _proj_gqa(x, norm_g, wt, qn_col, kn_col, tabs, S, *, x_fm):
    T = x.shape[1 if x_fm else 0]
    tm = TM_PROJ
    nb = S // tm
    rows = 2 * GQA_WIDTH + 2 * GQA_KV_WIDTH
    return pl.pallas_call(
        functools.partial(_proj_gqa_kernel, x_fm=x_fm),
        grid=(T // tm,),
        in_specs=_x_specs(tm, x_fm) + [
            _const_spec((rows, D_MODEL)),
            _const_spec((GQA_HEAD_DIM, 1)),
            _const_spec((GQA_HEAD_DIM, 1)),
            pl.BlockSpec((2 * GQA_HEAD_DIM, tm), lambda i: (0, i % nb)),
        ],
        out_specs=[
            pl.BlockSpec((GQA_WIDTH, tm), lambda i: (0, i)),
            pl.BlockSpec((GQA_KV_HEADS // 2, tm, 128), lambda i: (0, i, 0)),
            pl.BlockSpec((GQA_KV_WIDTH, tm), lambda i: (0, i)),
            pl.BlockSpec((GQA_WIDTH, tm), lambda i: (0, i)),
        ],
        out_shape=[
            jax.ShapeDtypeStruct((GQA_WIDTH, T), BF16),
            jax.ShapeDtypeStruct((GQA_KV_HEADS // 2, T, 128), BF16),
            jax.ShapeDtypeStruct((GQA_KV_WIDTH, T), BF16),
            jax.ShapeDtypeStruct((GQA_WIDTH, T), F32),
        ],
        compiler_params=_params(1),
        name="proj_gqa",
    )(x, norm_g, wt, qn_col, kn_col, tabs)


def _proj_mla_kernel(x_ref, g_ref, w_ref, qan_ref, wqb_ref, kvan_ref, wkvb_ref, tab_ref,
                     qt_ref, k_ref, vt_ref, gt_ref, *, x_fm):
    mm = _normed_matmul(x_ref, g_ref, x_fm)
    cos = tab_ref[0:MLA_ROPE, :]
    sin = tab_ref[MLA_ROPE:2 * MLA_ROPE, :]
    tm = tab_ref.shape[1]
    half = MLA_ROPE // 2
    scale = MLA_QK ** -0.5 * LOG2E

    def rope(xt):
        return xt * cos + _rot_half_cols(xt, half) * sin

    o = 0
    cq = _rms_cols(mm(w_ref[o:o + MLA_Q_LORA, :]), qan_ref[...]).astype(BF16)
    qt = _dot(wqb_ref[...], cq)
    zpad = jnp.zeros((MLA_QK_PAD - MLA_QK, tm), BF16)
    for hd in range(MLA_HEADS):
        b = MLA_QK * hd
        ob = MLA_QK_PAD * hd
        qt_ref[ob:ob + MLA_NOPE, :] = (qt[b:b + MLA_NOPE, :] * scale).astype(BF16)
        qt_ref[ob + MLA_NOPE:ob + MLA_QK, :] = (rope(qt[b + MLA_NOPE:b + MLA_QK, :]) * scale).astype(BF16)
        qt_ref[ob + MLA_QK:ob + MLA_QK_PAD, :] = zpad
    o += MLA_Q_LORA
    ckv = _rms_cols(mm(w_ref[o:o + MLA_KV_LORA, :]), kvan_ref[...]).astype(BF16)
    kvt = _dot(wkvb_ref[...], ckv)
    o += MLA_KV_LORA
    krt = rope(mm(w_ref[o:o + MLA_ROPE, :]))
    kr = jnp.concatenate([krt, jnp.zeros_like(krt)], axis=0).T.astype(BF16)
    o += MLA_ROPE
    for hd in range(MLA_HEADS):
        b = (MLA_NOPE + MLA_V) * hd
        k_ref[hd, :, 0:MLA_NOPE] = kvt[b:b + MLA_NOPE, :].T.astype(BF16)
        k_ref[hd, :, MLA_NOPE:MLA_QK_PAD] = kr
        vt_ref[MLA_V * hd:MLA_V * (hd + 1), :] = kvt[b + MLA_NOPE:b + MLA_NOPE + MLA_V, :].astype(BF16)
    gt_ref[...] = _silu(mm(w_ref[o:o + MLA_WIDTH, :]))


def _proj_mla(x, norm_g, wt, qan_col, wqb_t, kvan_col, wkvb_t, tabs, S, *, x_fm):
    T = x.shape[1 if x_fm else 0]
    tm = TM_PROJ
    nb = S // tm
    rows = MLA_Q_LORA + MLA_KV_LORA + MLA_ROPE + MLA_WIDTH
    return pl.pallas_call(
        functools.partial(_proj_mla_kernel, x_fm=x_fm),
        grid=(T // tm,),
        in_specs=_x_specs(tm, x_fm) + [
            _const_spec((rows, D_MODEL)),
            _const_spec((MLA_Q_LORA, 1)),
            _const_spec((MLA_HEADS * MLA_QK, MLA_Q_LORA)),
            _const_spec((MLA_KV_LORA, 1)),
            _const_spec((MLA_HEADS * (MLA_NOPE + MLA_V), MLA_KV_LORA)),
            pl.BlockSpec((2 * MLA_ROPE, tm), lambda i: (0, i % nb)),
        ],
        out_specs=[
            pl.BlockSpec((MLA_HEADS * MLA_QK_PAD, tm), lambda i: (0, i)),
            pl.BlockSpec((MLA_HEADS, tm, MLA_QK_PAD), lambda i: (0, i, 0)),
            pl.BlockSpec((MLA_WIDTH, tm), lambda i: (0, i)),
            pl.BlockSpec((MLA_WIDTH, tm), lambda i: (0, i)),
        ],
        out_shape=[
            jax.ShapeDtypeStruct((MLA_HEADS * MLA_QK_PAD, T), BF16),
            jax.ShapeDtypeStruct((MLA_HEADS, T, MLA_QK_PAD), BF16),
            jax.ShapeDtypeStruct((MLA_WIDTH, T), BF16),
            jax.ShapeDtypeStruct((MLA_WIDTH, T), F32),
        ],
        compiler_params=_params(1),
        name="proj_mla",
    )(x, norm_g, wt, qan_col, wqb_t, kvan_col, wkvb_t, tabs)


def _flash_kernel(qt_ref, k_ref, vt_ref, gt_ref, o_ref, s_ref, acc_ref, *, n_chunks, tk, unroll, q_half_of):
    g = pl.program_id(1)
    qt = qt_ref[...]
    if q_half_of is not None:
        half = q_half_of(g)
        z = jnp.zeros_like(qt)
        qt = jnp.concatenate([jnp.where(half == 0, qt, z), jnp.where(half == 1, qt, z)], axis=0)
    tq = qt.shape[1]
    dv = vt_ref.shape[0]
    acc_ref[...] = jnp.zeros_like(acc_ref)
    ones = jnp.ones((SUM_ROWS, tk), BF16)

    def stage1(c, slot):
        start = pl.multiple_of(c * tk, tk)
        s = _dot(k_ref[0, pl.ds(start, tk), :], qt)
        s_ref[slot] = s
        return jnp.max(s, axis=0, keepdims=True)

    def stage2(c, slot, m, m_cur):
        start = pl.multiple_of(c * tk, tk)
        m_new = jnp.maximum(m, m_cur)
        alpha = jnp.exp2(m - m_new)
        p = jnp.exp2(s_ref[slot] - m_new).astype(BF16)
        v1 = jnp.concatenate([vt_ref[:, pl.ds(start, tk)], ones], axis=0)
        acc_ref[...] = alpha * acc_ref[...] + _dot(v1, p)
        return m_new

    def group(i, carry):
        m, m_cur = carry
        c = unroll * i
        for u in range(unroll):
            m_nxt = stage1(c + u + 1, (u + 1) % 2)
            m = stage2(c + u, u % 2, m, m_cur)
            m_cur = m_nxt
        return m, m_cur

    n_groups = (n_chunks - 1) // unroll
    m = jnp.full((1, tq), -jnp.inf, F32)
    m_cur = stage1(0, 0)
    m, m_cur = lax.fori_loop(0, n_groups, group, (m, m_cur))
    for c in range(n_groups * unroll, n_chunks):
        m_nxt = stage1(c + 1, (c + 1) % 2) if c + 1 < n_chunks else None
        m = stage2(c, c % 2, m, m_cur)
        m_cur = m_nxt
    l = acc_ref[dv:dv + 1, :]
    o_ref[...] = (acc_ref[0:dv, :] * (1.0 / l) * gt_ref[...]).astype(BF16)


def _flash(qt, k, vt, gt, *, B, S, heads, dk, dv, kv_of, q_half_of=None, name):
    tq = S if S <= FLASH_TQ_SHORT else FLASH_TQ_LONG
    tk, unroll = FLASH_KEY_TILES[dv]
    nq = S // tq
    dkk = k.shape[2]
    n_chunks = S // tk
    assert S % tq == 0 and S % tk == 0 and unroll % 2 == 0
    kern = functools.partial(_flash_kernel, n_chunks=n_chunks, tk=tk, unroll=unroll, q_half_of=q_half_of)
    k_of, v_of = kv_of
    return pl.pallas_call(
        kern,
        grid=(B, heads, nq),
        in_specs=[
            pl.BlockSpec((dk, tq), lambda b, g, i: (g, b * nq + i)),
            pl.BlockSpec((1, S, dkk), lambda b, g, i: (k_of(g), b, 0)),
            pl.BlockSpec((dv, S), lambda b, g, i: (v_of(g), b)),
            pl.BlockSpec((dv, tq), lambda b, g, i: (g, b * nq + i)),
        ],
        out_specs=pl.BlockSpec((dv, tq), lambda b, g, i: (g, b * nq + i)),
        out_shape=jax.ShapeDtypeStruct((heads * dv, B * S), BF16),
        scratch_shapes=[pltpu.VMEM((2, tk, tq), F32), pltpu.VMEM((dv + SUM_ROWS, tq), F32)],
        compiler_params=_params(3),
        name=name,
    )(qt, k, vt, gt)


def _flash_mla(qt, k, vt, gt, *, B, S):
    return _flash(qt, k, vt, gt, B=B, S=S, heads=MLA_HEADS, dk=MLA_QK_PAD, dv=MLA_V,
                  kv_of=(lambda g: g, lambda g: g), name="flash_mla")


def _flash_gqa(qt, k, vt, gt, *, B, S):
    return _flash(qt, k, vt, gt, B=B, S=S, heads=GQA_HEADS, dk=GQA_HEAD_DIM, dv=GQA_HEAD_DIM,
                  kv_of=(lambda g: g // (2 * GQA_GROUP), lambda g: g // GQA_GROUP),
                  q_half_of=lambda g: (g // GQA_GROUP) % 2, name="flash_gqa")


def _na_kernel(qt_ref, k_ref, vt_ref, bias_ref, gt_ref, o_ref, *, rows, n_inner):
    i = pl.program_id(2)
    half_rows = rows // NA_ROWS_PER_STEP
    D = NA_HEAD_DIM
    top = lax.broadcasted_iota(jnp.int32, (2 * D, NA_QT), 0) < D
    ones = jnp.ones((SUM_ROWS, NA_KEYS), BF16)

    def scores(t):
        j = i * n_inner + t
        q = qt_ref[:, t * NA_QT:(t + 1) * NA_QT]
        z = jnp.zeros_like(q)
        q2 = jnp.concatenate([jnp.where(top, q, z), jnp.where(top, z, q)], axis=1)
        kb = jnp.clip(NA_ROWS_PER_STEP * j - NA_WIN_H // 2, 0, rows - NA_KEY_ROWS)
        ks = pl.multiple_of(kb * GRID_W, 2 * GRID_W)
        cls = jnp.where(j < 2, j, jnp.where(j > half_rows - 3, j - (half_rows - NA_CLASSES), 2))
        return _dot(k_ref[0, pl.ds(ks, NA_KEYS), :], q2) + bias_ref[0, cls], ks

    def finish(t, s, ks):
        m = jnp.max(s, axis=0, keepdims=True)
        p = jnp.exp(s - m).astype(BF16)
        for e in range(2):
            v1 = jnp.concatenate([vt_ref[D * e:D * (e + 1), pl.ds(ks, NA_KEYS)], ones], axis=0)
            y = _dot(v1, p[:, e * NA_QT:(e + 1) * NA_QT])
            g = gt_ref[D * e:D * (e + 1), t * NA_QT:(t + 1) * NA_QT]
            o_ref[D * e:D * (e + 1), t * NA_QT:(t + 1) * NA_QT] = (y[0:D] * (1.0 / y[D:D + 1]) * g).astype(BF16)

    cur = scores(0)
    for t in range(n_inner):
        nxt = scores(t + 1) if t + 1 < n_inner else None
        finish(t, *cur)
        cur = nxt


def _na_attn(qt, k, vt, bias, gt, *, B, S):
    rows = S // GRID_W
    tq = math.gcd(NA_TQ, S)
    assert tq % NA_QT == 0
    nq = S // tq
    n_inner = tq // NA_QT
    kern = functools.partial(_na_kernel, rows=rows, n_inner=n_inner)
    D2 = 2 * NA_HEAD_DIM
    return pl.pallas_call(
        kern,
        grid=(B, NA_HEADS // 2, nq),
        in_specs=[
            pl.BlockSpec((D2, tq), lambda b, h, i: (h, b * nq + i)),
            pl.BlockSpec((1, S, D2), lambda b, h, i: (h, b, 0)),
            pl.BlockSpec((D2, S), lambda b, h, i: (h, b)),
            pl.BlockSpec((1, NA_CLASSES, NA_KEYS, 2 * NA_QT), lambda b, h, i: (h, 0, 0, 0)),
            pl.BlockSpec((D2, tq), lambda b, h, i: (h, b * nq + i)),
        ],
        out_specs=pl.BlockSpec((D2, tq), lambda b, h, i: (h, b * nq + i)),
        out_shape=jax.ShapeDtypeStruct((NA_WIDTH, B * S), BF16),
        compiler_params=_params(3),
        name="na_attn",
    )(qt, k, vt, bias, gt)


def _na_bias_table(rpb, rows=2 * NA_KEY_ROWS):
    hr = rows // NA_ROWS_PER_STEP
    reps = [0, 1, 2, hr - 2, hr - 1]
    i = np.arange(NA_KEY_ROWS)[:, None, None, None]
    kc = np.arange(GRID_W)[None, :, None, None]
    rr = np.arange(NA_ROWS_PER_STEP)[None, None, :, None]
    qc = np.arange(GRID_W)[None, None, None, :]
    cs = np.clip(qc - NA_WIN_W // 2, 0, GRID_W - NA_WIN_W)
    col_ok = (kc >= cs) & (kc < cs + NA_WIN_W)
    di_sel, ok_all = [], []
    for j in reps:
        r = NA_ROWS_PER_STEP * j + rr
        rs = np.clip(r - NA_WIN_H // 2, 0, rows - NA_WIN_H)
        kb = np.clip(NA_ROWS_PER_STEP * j - NA_WIN_H // 2, 0, rows - NA_KEY_ROWS)
        kr = kb + i
        row_ok = (kr >= rs) & (kr < rs + NA_WIN_H)
        di = np.clip(kr - r + (NA_WIN_H - 1), 0, 2 * NA_WIN_H - 2)
        di_sel.extend(int(v) for v in di.reshape(-1))
        ok_all.append(np.broadcast_to(row_ok & col_ok, (NA_KEY_ROWS, GRID_W, NA_ROWS_PER_STEP, GRID_W)))
    ok = np.stack(ok_all).reshape(NA_CLASSES, NA_KEYS, NA_QT)
    H, nd, nw = rpb.shape
    lo = (GRID_W - 1) - (NA_WIN_W - 1)
    r128 = jnp.pad(rpb, ((0, 0), (0, 0), (lo, 128 - nw - lo)))
    y = jnp.tile(r128, (1, 1, GRID_W))[:, :, :GRID_W * 127].reshape(H, nd, GRID_W, 127)
    toep = y[:, :, :, GRID_W - 1:2 * GRID_W - 1]
    sel = jnp.stack([toep[:, d] for d in di_sel], axis=1)
    sel = sel.reshape(H, NA_CLASSES, NA_KEY_ROWS, NA_ROWS_PER_STEP, GRID_W, GRID_W)
    vals = sel.transpose(0, 1, 2, 5, 3, 4).reshape(H, NA_CLASSES, NA_KEYS, NA_QT)
    tab = jnp.where(ok[None], vals, NEG_BIG).astype(F32)
    tab = tab.reshape(H // 2, 2, NA_CLASSES, NA_KEYS, NA_QT).transpose(0, 2, 3, 1, 4)
    return tab.reshape(H // 2, NA_CLASSES, NA_KEYS, 2 * NA_QT)


def _out_kernel(zna_ref, zmla_ref, zgqa_ref, x_ref, p_ref, wna_ref, wmla_ref, wgqa_ref,
                pn_ref, wpg_ref, wpe_ref, fn_ref, o_ref, x1_ref, hn_ref, *, x_fm, final):
    tm = x1_ref.shape[1]
    chunks = [slice(r, r + OUT_ROW_CHUNK) for r in range(0, D_MODEL, OUT_ROW_CHUNK)]
    xt = x_ref[...] if x_fm else x_ref[...].T
    ss = jnp.zeros((1, tm), F32)
    for r in chunks:
        x1 = xt[r, :] + (_dot(wna_ref[r, :], zna_ref[...]) + _dot(wmla_ref[r, :], zmla_ref[...])
                         + _dot(wgqa_ref[r, :], zgqa_ref[...]))
        x1_ref[r, :] = x1
        ss = ss + jnp.sum(x1 * x1, axis=0, keepdims=True)
    inv = lax.rsqrt(ss * (1.0 / D_MODEL) + EPS)
    for r in chunks:
        hn_ref[r, :] = (x1_ref[r, :] * inv * pn_ref[r, :]).astype(BF16)
    pb = p_ref[...].astype(BF16)
    ss = jnp.zeros((1, tm), F32)
    for r in chunks:
        gate = 1.0 / (1.0 + jnp.exp(-_dot(wpg_ref[r, :], hn_ref[...])))
        x2 = x1_ref[r, :] + _dot_nt(wpe_ref[r, :], pb) * gate
        if final:
            x1_ref[r, :] = x2
            ss = ss + jnp.sum(x2 * x2, axis=0, keepdims=True)
        else:
            o_ref[r, :] = x2
    if final:
        inv = lax.rsqrt(ss * (1.0 / D_MODEL) + EPS)
        for r in chunks:
            o_ref[:, r] = (x1_ref[r, :] * inv * fn_ref[r, :]).T


def _out_ple(zna, zmla, zgqa, x, p, wo_na, wo_mla, wo_gqa, pn_col, wpg_t, wpe_t, fn_col, *, x_fm, final):
    T = p.shape[0]
    tm = TM_OUT
    x_spec = pl.BlockSpec((D_MODEL, tm), lambda i: (0, i)) if x_fm else pl.BlockSpec((tm, D_MODEL), lambda i: (i, 0))
    if final:
        out_spec, out_shape = pl.BlockSpec((tm, D_MODEL), lambda i: (i, 0)), (T, D_MODEL)
    else:
        out_spec, out_shape = pl.BlockSpec((D_MODEL, tm), lambda i: (0, i)), (D_MODEL, T)
    return pl.pallas_call(
        functools.partial(_out_kernel, x_fm=x_fm, final=final),
        grid=(T // tm,),
        in_specs=[
            pl.BlockSpec((NA_WIDTH, tm), lambda i: (0, i)),
            pl.BlockSpec((MLA_WIDTH, tm), lambda i: (0, i)),
            pl.BlockSpec((GQA_WIDTH, tm), lambda i: (0, i)),
            x_spec,
            pl.BlockSpec((tm, PLE_DIM), lambda i: (i, 0)),
            _const_spec((D_MODEL, NA_WIDTH)),
            _const_spec((D_MODEL, MLA_WIDTH)),
            _const_spec((D_MODEL, GQA_WIDTH)),
            _const_spec((D_MODEL, 1)),
            _const_spec((D_MODEL, D_MODEL)),
            _const_spec((D_MODEL, PLE_DIM)),
            _const_spec((D_MODEL, 1)),
        ],
        out_specs=out_spec,
        out_shape=jax.ShapeDtypeStruct(out_shape, F32),
        scratch_shapes=[pltpu.VMEM((D_MODEL, tm), F32), pltpu.VMEM((D_MODEL, tm), BF16)],
        compiler_params=_params(1),
        name="out_ple",
    )(zna, zmla, zgqa, x, p, wo_na, wo_mla, wo_gqa, pn_col, wpg_t, wpe_t, fn_col)


def _rope_tables_t(pos, dim):
    inv = ROPE_THETA ** (-jnp.arange(0, dim, 2, dtype=F32) / dim)
    ang = pos.astype(F32)[:, None] * inv[None, :]
    ang = jnp.concatenate([ang, ang], axis=-1)
    return jnp.cos(ang).T, jnp.sin(ang).T


def _tables(S):
    t = jnp.arange(S)
    cos_t, sin_t = _rope_tables_t(t, MLA_ROPE)
    cos_r, sin_r = _rope_tables_t(t // GRID_W, GQA_HEAD_DIM // 2)
    cos_c, sin_c = _rope_tables_t(t % GRID_W, GQA_HEAD_DIM // 2)
    return (jnp.concatenate([cos_t, sin_t], axis=0),
            jnp.concatenate([cos_r, cos_c, sin_r, sin_c], axis=0))


def _layer_params(i, norm_g, w_in, na_rpb, mla_q_a_norm, mla_w_q_b, mla_kv_a_norm, mla_w_kv_b,
                  gqa_q_norm, gqa_k_norm, w_out, ple_norm, w_pe, w_pg):
    wt = w_in[i].T
    o = 0
    w_na = wt[o:o + 4 * NA_WIDTH]
    w_na = jnp.concatenate([w_na[:NA_WIDTH] * (NA_HEAD_DIM ** -0.5), w_na[NA_WIDTH:]], axis=0)
    o += 4 * NA_WIDTH
    n_mla = MLA_Q_LORA + MLA_KV_LORA + MLA_ROPE + MLA_WIDTH
    w_mla = wt[o:o + n_mla]
    o += n_mla
    w_gqa = wt[o:]
    wo_t = w_out[i].T
    return dict(
        norm_g=norm_g[i][None, :], norm_g_col=norm_g[i][:, None],
        w_na=w_na.astype(BF16), w_mla=w_mla.astype(BF16), w_gqa=w_gqa.astype(BF16),
        qan=mla_q_a_norm[i][:, None], wqb_t=mla_w_q_b[i].T.astype(BF16),
        kvan=mla_kv_a_norm[i][:, None], wkvb_t=mla_w_kv_b[i].T.astype(BF16),
        qn=gqa_q_norm[i][:, None], kn=gqa_k_norm[i][:, None],
        bias=_na_bias_table(na_rpb[i]),
        wo_na=wo_t[:, :NA_WIDTH].astype(BF16),
        wo_mla=wo_t[:, NA_WIDTH:NA_WIDTH + MLA_WIDTH].astype(BF16),
        wo_gqa=wo_t[:, NA_WIDTH + MLA_WIDTH:].astype(BF16),
        ple_norm=ple_norm[i][:, None], wpg_t=w_pg[i].T.astype(BF16), wpe_t=w_pe[i].T.astype(BF16),
    )


def _run_trunk(x, p, layers, final_norm):
    B, S, _ = x.shape
    T = B * S
    rows = S // GRID_W
    assert S % TM_PROJ == 0 and S % TM_OUT == 0 and S % GRID_W == 0
    assert rows >= NA_KEY_ROWS and rows % NA_ROWS_PER_STEP == 0 and rows // NA_ROWS_PER_STEP >= NA_CLASSES
    tab_mla, tab_gqa = _tables(S)
    xf = x.reshape(T, D_MODEL)
    fn = final_norm[:, None]
    depth = len(layers)
    for i, L in enumerate(layers):
        x_fm = i > 0
        g = L["norm_g_col"] if x_fm else L["norm_g"]
        pf = p[i].reshape(T, PLE_DIM)
        na_qt, na_k, na_vt, na_gt = _proj_na(xf, g, L["w_na"], x_fm=x_fm)
        ml_qt, ml_k, ml_vt, ml_gt = _proj_mla(xf, g, L["w_mla"], L["qan"], L["wqb_t"],
                                              L["kvan"], L["wkvb_t"], tab_mla, S, x_fm=x_fm)
        gq_qt, gq_k, gq_vt, gq_gt = _proj_gqa(xf, g, L["w_gqa"], L["qn"], L["kn"], tab_gqa, S, x_fm=x_fm)
        z_na = _na_attn(na_qt, na_k, na_vt, L["bias"], na_gt, B=B, S=S)
        z_mla = _flash_mla(ml_qt, ml_k, ml_vt, ml_gt, B=B, S=S)
        z_gqa = _flash_gqa(gq_qt, gq_k, gq_vt, gq_gt, B=B, S=S)
        xf = _out_ple(z_na, z_mla, z_gqa, xf, pf, L["wo_na"], L["wo_mla"], L["wo_gqa"], L["ple_norm"],
                      L["wpg_t"], L["wpe_t"], fn, x_fm=x_fm, final=(i == depth - 1))
    return xf.reshape(B, S, D_MODEL)


def kernel(x_prompt, x_sample, p_prompt, p_sample, norm_g, w_in, na_rpb, mla_q_a_norm, mla_w_q_b,
           mla_kv_a_norm, mla_w_kv_b, gqa_q_norm, gqa_k_norm, w_out, ple_norm, w_pe, w_pg, final_norm):
    depth = w_in.shape[0]
    layers = [_layer_params(i, norm_g, w_in, na_rpb, mla_q_a_norm, mla_w_q_b, mla_kv_a_norm, mla_w_kv_b,
                            gqa_q_norm, gqa_k_norm, w_out, ple_norm, w_pe, w_pg) for i in range(depth)]
    y_prompt = _run_trunk(x_prompt, p_prompt, layers, final_norm)
    y_sample = _run_trunk(x_sample, p_sample, layers, final_norm)
    return (y_prompt, y_sample)
```

```python
import functools
import math

import jax
import jax.numpy as jnp
from jax import lax
from jax.experimental import pallas as pl
from jax.experimental.pallas import tpu as pltpu

F32 = jnp.float32
BF16 = jnp.bfloat16

D_MODEL = 2048
GRID_W = 64
PLE_DIM = 256
EPS = 1e-6
ROPE_THETA = 10000.0

NA_HEADS, NA_HEAD_DIM, NA_WIN_H, NA_WIN_W = 8, 64, 8, 16
NA_WIDTH = NA_HEADS * NA_HEAD_DIM
MLA_HEADS, MLA_NOPE, MLA_ROPE, MLA_V = 6, 128, 64, 128
MLA_Q_LORA, MLA_KV_LORA = 512, 256
MLA_WIDTH = MLA_HEADS * MLA_V
MLA_QK = MLA_NOPE + MLA_ROPE
MLA_QK_PAD = 256
GQA_HEADS, GQA_KV_HEADS, GQA_HEAD_DIM = 12, 4, 64
GQA_GROUP = GQA_HEADS // GQA_KV_HEADS
GQA_WIDTH = GQA_HEADS * GQA_HEAD_DIM
GQA_KV_WIDTH = GQA_KV_HEADS * GQA_HEAD_DIM

VMEM_LIMIT_BYTES = 56 * 1024 * 1024

TM_PROJ = 1024
TM_OUT = 512
OUT_ROW_CHUNK = 512
SUM_ROWS = 16
FLASH_TQ_LONG = 1024
FLASH_TQ_SHORT = 2048
FLASH_KEY_TILES = {GQA_HEAD_DIM: (256, 16), MLA_V: (512, 4)}
NA_ROWS_PER_STEP = 4
NA_KEY_ROWS = 12
NA_KEYS = NA_KEY_ROWS * GRID_W
NA_QT = NA_ROWS_PER_STEP * GRID_W
NA_CLASSES = 3
NA_TQ = 2048
NEG_BIG = -1e30
LOG2E = 1.4426950408889634

_NT = (((1,), (1,)), ((), ()))


def _dot(a, b):
    return jnp.dot(a, b, preferred_element_type=F32)


def _dot_nt(a, b):
    return lax.dot_general(a, b, _NT, preferred_element_type=F32)


def _params(n_axes):
    return pltpu.CompilerParams(
        dimension_semantics=("arbitrary",) * n_axes,
        vmem_limit_bytes=VMEM_LIMIT_BYTES,
    )


def _const_spec(shape):
    nd = len(shape)
    return pl.BlockSpec(shape, lambda *_: (0,) * nd, pipeline_mode=pl.Buffered(1))


def _silu(x):
    return x * (1.0 / (1.0 + jnp.exp(-x)))


def _rms_rows(x, g_row):
    ms = jnp.mean(x * x, axis=-1, keepdims=True)
    return x * lax.rsqrt(ms + EPS) * g_row


def _rms_cols(xt, g_col):
    ms = jnp.mean(xt * xt, axis=0, keepdims=True)
    return xt * lax.rsqrt(ms + EPS) * g_col


def _normed_matmul(x_ref, g_ref, x_fm):
    if x_fm:
        ht = _rms_cols(x_ref[...], g_ref[...]).astype(BF16)
        return lambda w: _dot(w, ht)
    h = _rms_rows(x_ref[...], g_ref[...]).astype(BF16)
    return lambda w: _dot_nt(w, h)


def _x_specs(tm, x_fm):
    if x_fm:
        return [pl.BlockSpec((D_MODEL, tm), lambda i: (0, i)), _const_spec((D_MODEL, 1))]
    return [pl.BlockSpec((tm, D_MODEL), lambda i: (i, 0)), _const_spec((1, D_MODEL))]


def _rot_half_cols(xt, n):
    return jnp.concatenate([-xt[n:2 * n], xt[0:n]], axis=0)


def _proj_na_kernel(x_ref, g_ref, w_ref, qt_ref, k_ref, vt_ref, gt_ref, *, x_fm):
    mm = _normed_matmul(x_ref, g_ref, x_fm)
    W = NA_WIDTH
    qt_ref[...] = (mm(w_ref[0:W, :]) * (NA_HEAD_DIM ** -0.5 * LOG2E)).astype(BF16)
    kt = mm(w_ref[W:2 * W, :])
    for p in range(NA_HEADS // 2):
        k_ref[p] = kt[128 * p:128 * (p + 1), :].T.astype(BF16)
    vt_ref[...] = mm(w_ref[2 * W:3 * W, :]).astype(BF16)
    gt_ref[...] = _silu(mm(w_ref[3 * W:4 * W, :]))


def _proj_na(x, norm_g, wt, *, x_fm):
    T = x.shape[1 if x_fm else 0]
    tm = TM_PROJ
    return pl.pallas_call(
        functools.partial(_proj_na_kernel, x_fm=x_fm),
        grid=(T // tm,),
        in_specs=_x_specs(tm, x_fm) + [
            _const_spec((4 * NA_WIDTH, D_MODEL)),
        ],
        out_specs=[
            pl.BlockSpec((NA_WIDTH, tm), lambda i: (0, i)),
            pl.BlockSpec((NA_HEADS // 2, tm, 128), lambda i: (0, i, 0)),
            pl.BlockSpec((NA_WIDTH, tm), lambda i: (0, i)),
            pl.BlockSpec((NA_WIDTH, tm), lambda i: (0, i)),
        ],
        out_shape=[
            jax.ShapeDtypeStruct((NA_WIDTH, T), BF16),
            jax.ShapeDtypeStruct((NA_HEADS // 2, T, 128), BF16),
            jax.ShapeDtypeStruct((NA_WIDTH, T), BF16),
            jax.ShapeDtypeStruct((NA_WIDTH, T), F32),
        ],
        compiler_params=_params(1),
        name="proj_na",
    )(x, norm_g, wt)


def _proj_gqa_kernel(x_ref, g_ref, w_ref, qn_ref, kn_ref, tab_ref, qt_ref, k_ref, vt_ref, gt_ref, *, x_fm):
    mm = _normed_matmul(x_ref, g_ref, x_fm)
    cos = tab_ref[0:GQA_HEAD_DIM, :]
    sin = tab_ref[GQA_HEAD_DIM:2 * GQA_HEAD_DIM, :]
    q4 = GQA_HEAD_DIM // 4

    def norm_rope(xt, g_col):
        y = _rms_cols(xt, g_col)
        rot = jnp.concatenate([-y[q4:2 * q4], y[0:q4], -y[3 * q4:4 * q4], y[2 * q4:3 * q4]], axis=0)
        return y * cos + rot * sin

    o = 0
    qt = mm(w_ref[o:o + GQA_WIDTH, :])
    for hd in range(GQA_HEADS):
        r = slice(GQA_HEAD_DIM * hd, GQA_HEAD_DIM * (hd + 1))
        qt_ref[r, :] = (norm_rope(qt[r, :], qn_ref[...]) * (GQA_HEAD_DIM ** -0.5 * LOG2E)).astype(BF16)
    o += GQA_WIDTH
    kt = mm(w_ref[o:o + GQA_KV_WIDTH, :])
    for p in range(GQA_KV_HEADS // 2):
        pair = jnp.concatenate(
            [norm_rope(kt[GQA_HEAD_DIM * (2 * p + e):GQA_HEAD_DIM * (2 * p + e + 1), :], kn_ref[...])
             for e in range(2)], axis=0)
        k_ref[p] = pair.T.astype(BF16)
    o += GQA_KV_WIDTH
    vt_ref[...] = mm(w_ref[o:o + GQA_KV_WIDTH, :]).astype(BF16)
    o += GQA_KV_WIDTH
    gt_ref[...] = _silu(mm(w_ref[o:o + GQA_WIDTH, :]))


def _proj_gqa(x, norm_g, wt, qn_col, kn_col, tabs, S, *, x_fm):
    T = x.shape[1 if x_fm else 0]
    tm = TM_PROJ
    nb = S // tm
    rows = 2 * GQA_WIDTH + 2 * GQA_KV_WIDTH
    return pl.pallas_call(
        functools.partial(_proj_gqa_kernel, x_fm=x_fm),
        grid=(T // tm,),
        in_specs=_x_specs(tm, x_fm) + [
            _const_spec((rows, D_MODEL)),
            _const_spec((GQA_HEAD_DIM, 1)),
            _const_spec((GQA_HEAD_DIM, 1)),
            pl.BlockSpec((2 * GQA_HEAD_DIM, tm), lambda i: (0, i % nb)),
        ],
        out_specs=[
            pl.BlockSpec((GQA_WIDTH, tm), lambda i: (0, i)),
            pl.BlockSpec((GQA_KV_HEADS // 2, tm, 128), lambda i: (0, i, 0)),
            pl.BlockSpec((GQA_KV_WIDTH, tm), lambda i: (0, i)),
            pl.BlockSpec((GQA_WIDTH, tm), lambda i: (0, i)),
        ],
        out_shape=[
            jax.ShapeDtypeStruct((GQA_WIDTH, T), BF16),
            jax.ShapeDtypeStruct((GQA_KV_HEADS // 2, T, 128), BF16),
            jax.ShapeDtypeStruct((GQA_KV_WIDTH, T), BF16),
            jax.ShapeDtypeStruct((GQA_WIDTH, T), F32),
        ],
        compiler_params=_params(1),
        name="proj_gqa",
    )(x, norm_g, wt, qn_col, kn_col, tabs)


def _proj_mla_kernel(x_ref, g_ref, w_ref, qan_ref, wqb_ref, kvan_ref, wkvb_ref, tab_ref,
                     qt_ref, k_ref, vt_ref, gt_ref, *, x_fm):
    mm = _normed_matmul(x_ref, g_ref, x_fm)
    cos = tab_ref[0:MLA_ROPE, :]
    sin = tab_ref[MLA_ROPE:2 * MLA_ROPE, :]
    tm = tab_ref.shape[1]
    half = MLA_ROPE // 2
    scale = MLA_QK ** -0.5 * LOG2E

    def rope(xt):
        return xt * cos + _rot_half_cols(xt, half) * sin

    o = 0
    cq = _rms_cols(mm(w_ref[o:o + MLA_Q_LORA, :]), qan_ref[...]).astype(BF16)
    qt = _dot(wqb_ref[...], cq)
    zpad = jnp.zeros((MLA_QK_PAD - MLA_QK, tm), BF16)
    for hd in range(MLA_HEADS):
        b = MLA_QK * hd
        ob = MLA_QK_PAD * hd
        qt_ref[ob:ob + MLA_NOPE, :] = (qt[b:b + MLA_NOPE, :] * scale).astype(BF16)
        qt_ref[ob + MLA_NOPE:ob + MLA_QK, :] = (rope(qt[b + MLA_NOPE:b + MLA_QK, :]) * scale).astype(BF16)
        qt_ref[ob + MLA_QK:ob + MLA_QK_PAD, :] = zpad
    o += MLA_Q_LORA
    ckv = _rms_cols(mm(w_ref[o:o + MLA_KV_LORA, :]), kvan_ref[...]).astype(BF16)
    kvt = _dot(wkvb_ref[...], ckv)
    o += MLA_KV_LORA
    krt = rope(mm(w_ref[o:o + MLA_ROPE, :]))
    kr = jnp.concatenate([krt, jnp.zeros_like(krt)], axis=0).T.astype(BF16)
    o += MLA_ROPE
    for hd in range(MLA_HEADS):
        b = (MLA_NOPE + MLA_V) * hd
        k_ref[hd, :, 0:MLA_NOPE] = kvt[b:b + MLA_NOPE, :].T.astype(BF16)
        k_ref[hd, :, MLA_NOPE:MLA_QK_PAD] = kr
        vt_ref[MLA_V * hd:MLA_V * (hd + 1), :] = kvt[b + MLA_NOPE:b + MLA_NOPE + MLA_V, :].astype(BF16)
    gt_ref[...] = _silu(mm(w_ref[o:o + MLA_WIDTH, :]))


def _proj_mla(x, norm_g, wt, qan_col, wqb_t, kvan_col, wkvb_t, tabs, S, *, x_fm):
    T = x.shape[1 if x_fm else 0]
    tm = TM_PROJ
    nb = S // tm
    rows = MLA_Q_LORA + MLA_KV_LORA + MLA_ROPE + MLA_WIDTH
    return pl.pallas_call(
        functools.partial(_proj_mla_kernel, x_fm=x_fm),
        grid=(T // tm,),
        in_specs=_x_specs(tm, x_fm) + [
            _const_spec((rows, D_MODEL)),
            _const_spec((MLA_Q_LORA, 1)),
            _const_spec((MLA_HEADS * MLA_QK, MLA_Q_LORA)),
            _const_spec((MLA_KV_LORA, 1)),
            _const_spec((MLA_HEADS * (MLA_NOPE + MLA_V), MLA_KV_LORA)),
            pl.BlockSpec((2 * MLA_ROPE, tm), lambda i: (0, i % nb)),
        ],
        out_specs=[
            pl.BlockSpec((MLA_HEADS * MLA_QK_PAD, tm), lambda i: (0, i)),
            pl.BlockSpec((MLA_HEADS, tm, MLA_QK_PAD), lambda i: (0, i, 0)),
            pl.BlockSpec((MLA_WIDTH, tm), lambda i: (0, i)),
            pl.BlockSpec((MLA_WIDTH, tm), lambda i: (0, i)),
        ],
        out_shape=[
            jax.ShapeDtypeStruct((MLA_HEADS * MLA_QK_PAD, T), BF16),
            jax.ShapeDtypeStruct((MLA_HEADS, T, MLA_QK_PAD), BF16),
            jax.ShapeDtypeStruct((MLA_WIDTH, T), BF16),
            jax.ShapeDtypeStruct((MLA_WIDTH, T), F32),
        ],
        compiler_params=_params(1),
        name="proj_mla",
    )(x, norm_g, wt, qan_col, wqb_t, kvan_col, wkvb_t, tabs)


def _flash_kernel(qt_ref, k_ref, vt_ref, gt_ref, o_ref, s_ref, acc_ref, *, n_chunks, tk, unroll, q_half_of):
    g = pl.program_id(1)
    qt = qt_ref[...]
    if q_half_of is not None:
        half = q_half_of(g)
        z = jnp.zeros_like(qt)
        qt = jnp.concatenate([jnp.where(half == 0, qt, z), jnp.where(half == 1, qt, z)], axis=0)
    tq = qt.shape[1]
    dv = vt_ref.shape[0]
    acc_ref[...] = jnp.zeros_like(acc_ref)
    ones = jnp.ones((SUM_ROWS, tk), BF16)

    def stage1(c, slot):
        start = pl.multiple_of(c * tk, tk)
        s = _dot(k_ref[0, pl.ds(start, tk), :], qt)
        s_ref[slot] = s
        return jnp.max(s, axis=0, keepdims=True)

    def stage2(c, slot, m, m_cur):
        start = pl.multiple_of(c * tk, tk)
        m_new = jnp.maximum(m, m_cur)
        alpha = jnp.exp2(m - m_new)
        p = jnp.exp2(s_ref[slot] - m_new).astype(BF16)
        v1 = jnp.concatenate([vt_ref[:, pl.ds(start, tk)], ones], axis=0)
        acc_ref[...] = alpha * acc_ref[...] + _dot(v1, p)
        return m_new

    def group(i, carry):
        m, m_cur = carry
        c = unroll * i
        for u in range(unroll):
            m_nxt = stage1(c + u + 1, (u + 1) % 2)
            m = stage2(c + u, u % 2, m, m_cur)
            m_cur = m_nxt
        return m, m_cur

    n_groups = (n_chunks - 1) // unroll
    m = jnp.full((1, tq), -jnp.inf, F32)
    m_cur = stage1(0, 0)
    m, m_cur = lax.fori_loop(0, n_groups, group, (m, m_cur))
    for c in range(n_groups * unroll, n_chunks):
        m_nxt = stage1(c + 1, (c + 1) % 2) if c + 1 < n_chunks else None
        m = stage2(c, c % 2, m, m_cur)
        m_cur = m_nxt
    l = acc_ref[dv:dv + 1, :]
    o_ref[...] = (acc_ref[0:dv, :] * (1.0 / l) * gt_ref[...]).astype(BF16)


def _flash(qt, k, vt, gt, *, B, S, heads, dk, dv, kv_of, q_half_of=None, name):
    tq = S if S <= FLASH_TQ_SHORT else FLASH_TQ_LONG
    tk, unroll = FLASH_KEY_TILES[dv]
    nq = S // tq
    dkk = k.shape[2]
    n_chunks = S // tk
    assert S % tq == 0 and S % tk == 0 and unroll % 2 == 0
    kern = functools.partial(_flash_kernel, n_chunks=n_chunks, tk=tk, unroll=unroll, q_half_of=q_half_of)
    k_of, v_of = kv_of
    return pl.pallas_call(
        kern,
        grid=(B, heads, nq),
        in_specs=[
            pl.BlockSpec((dk, tq), lambda b, g, i: (g, b * nq + i)),
            pl.BlockSpec((1, S, dkk), lambda b, g, i: (k_of(g), b, 0)),
            pl.BlockSpec((dv, S), lambda b, g, i: (v_of(g), b)),
            pl.BlockSpec((dv, tq), lambda b, g, i: (g, b * nq + i)),
        ],
        out_specs=pl.BlockSpec((dv, tq), lambda b, g, i: (g, b * nq + i)),
        out_shape=jax.ShapeDtypeStruct((heads * dv, B * S), BF16),
        scratch_shapes=[pltpu.VMEM((2, tk, tq), F32), pltpu.VMEM((dv + SUM_ROWS, tq), F32)],
        compiler_params=_params(3),
        name=name,
    )(qt, k, vt, gt)


def _flash_mla(qt, k, vt, gt, *, B, S):
    return _flash(qt, k, vt, gt, B=B, S=S, heads=MLA_HEADS, dk=MLA_QK_PAD, dv=MLA_V,
                  kv_of=(lambda g: g, lambda g: g), name="flash_mla")


def _flash_gqa(qt, k, vt, gt, *, B, S):
    return _flash(qt, k, vt, gt, B=B, S=S, heads=GQA_HEADS, dk=GQA_HEAD_DIM, dv=GQA_HEAD_DIM,
                  kv_of=(lambda g: g // (2 * GQA_GROUP), lambda g: g // GQA_GROUP),
                  q_half_of=lambda g: (g // GQA_GROUP) % 2, name="flash_gqa")


def _na_kernel(qt_ref, k_ref, vt_ref, bias_ref, gt_ref, o_ref, *, rows, n_inner):
    i = pl.program_id(2)
    half_rows = rows // NA_ROWS_PER_STEP
    D = NA_HEAD_DIM
    top = lax.broadcasted_iota(jnp.int32, (2 * D, NA_QT), 0) < D
    ones = jnp.ones((SUM_ROWS, NA_KEYS), BF16)

    def scores(t):
        j = i * n_inner + t
        q = qt_ref[:, t * NA_QT:(t + 1) * NA_QT]
        z = jnp.zeros_like(q)
        q2 = jnp.concatenate([jnp.where(top, q, z), jnp.where(top, z, q)], axis=1)
        kb = jnp.clip(NA_ROWS_PER_STEP * j - NA_WIN_H // 2, 0, rows - NA_KEY_ROWS)
        ks = pl.multiple_of(kb * GRID_W, 2 * GRID_W)
        cls = jnp.where(j == 0, 0, jnp.where(j == half_rows - 1, 2, 1))
        return _dot(k_ref[0, pl.ds(ks, NA_KEYS), :], q2) + bias_ref[0, cls], ks

    def finish(t, s, ks):
        m = jnp.max(s, axis=0, keepdims=True)
        p = jnp.exp2(s - m).astype(BF16)
        for e in range(2):
            v1 = jnp.concatenate([vt_ref[D * e:D * (e + 1), pl.ds(ks, NA_KEYS)], ones], axis=0)
            y = _dot(v1, p[:, e * NA_QT:(e + 1) * NA_QT])
            g = gt_ref[D * e:D * (e + 1), t * NA_QT:(t + 1) * NA_QT]
            o_ref[D * e:D * (e + 1), t * NA_QT:(t + 1) * NA_QT] = (y[0:D] * (1.0 / y[D:D + 1]) * g).astype(BF16)

    cur = scores(0)
    for t in range(n_inner):
        nxt = scores(t + 1) if t + 1 < n_inner else None
        finish(t, *cur)
        cur = nxt


def _na_attn(qt, k, vt, bias, gt, *, B, S):
    rows = S // GRID_W
    tq = math.gcd(NA_TQ, S)
    assert tq % NA_QT == 0
    nq = S // tq
    n_inner = tq // NA_QT
    kern = functools.partial(_na_kernel, rows=rows, n_inner=n_inner)
    D2 = 2 * NA_HEAD_DIM
    return pl.pallas_call(
        kern,
        grid=(B, NA_HEADS // 2, nq),
        in_specs=[
            pl.BlockSpec((D2, tq), lambda b, h, i: (h, b * nq + i)),
            pl.BlockSpec((1, S, D2), lambda b, h, i: (h, b, 0)),
            pl.BlockSpec((D2, S), lambda b, h, i: (h, b)),
            pl.BlockSpec((1, NA_CLASSES, NA_KEYS, 2 * NA_QT), lambda b, h, i: (h, 0, 0, 0)),
            pl.BlockSpec((D2, tq), lambda b, h, i: (h, b * nq + i)),
        ],
        out_specs=pl.BlockSpec((D2, tq), lambda b, h, i: (h, b * nq + i)),
        out_shape=jax.ShapeDtypeStruct((NA_WIDTH, B * S), BF16),
        compiler_params=_params(3),
        name="na_attn",
    )(qt, k, vt, bias, gt)


def _na_bias_kernel(toep_ref, o_ref):
    c = pl.program_id(1)
    off = jnp.where(c == 0, NA_WIN_H - 1, jnp.where(c == 1, NA_WIN_H - 1 - NA_ROWS_PER_STEP, -1))
    kc = lax.broadcasted_iota(jnp.int32, (GRID_W, GRID_W), 0)
    qc = lax.broadcasted_iota(jnp.int32, (GRID_W, GRID_W), 1)
    cs = jnp.clip(qc - NA_WIN_W // 2, 0, GRID_W - NA_WIN_W)
    col_ok = (kc >= cs) & (kc < cs + NA_WIN_W)
    for i in range(NA_KEY_ROWS):
        for e in range(2):
            blocks = []
            for rr in range(NA_ROWS_PER_STEP):
                lo = jnp.where(c == 0, 0, jnp.where(c == 1, rr, NA_KEY_ROWS - NA_WIN_H))
                row_ok = (i >= lo) & (i < lo + NA_WIN_H)
                d = jnp.clip(i - rr + off, 0, 2 * NA_WIN_H - 2)
                blocks.append(jnp.where(col_ok & row_ok, toep_ref[e, d], NEG_BIG))
            o_ref[0, 0, i * GRID_W:(i + 1) * GRID_W, e * NA_QT:(e + 1) * NA_QT] = jnp.concatenate(blocks, axis=1)


def _na_bias_table(rpb):
    assert (NA_ROWS_PER_STEP, NA_KEY_ROWS, NA_WIN_H) == (4, 12, 8)
    H, nd, nw = rpb.shape
    lo = (GRID_W - 1) - (NA_WIN_W - 1)
    r128 = jnp.pad(rpb * LOG2E, ((0, 0), (0, 0), (lo, 128 - nw - lo)))
    y = jnp.tile(r128, (1, 1, GRID_W))[:, :, :GRID_W * 127].reshape(H, nd, GRID_W, 127)
    toep = jnp.swapaxes(y[:, :, :, GRID_W - 1:2 * GRID_W - 1], 2, 3)
    return pl.pallas_call(
        _na_bias_kernel,
        grid=(H // 2, NA_CLASSES),
        in_specs=[pl.BlockSpec((2, nd, GRID_W, GRID_W), lambda p, c: (p, 0, 0, 0))],
        out_specs=pl.BlockSpec((1, 1, NA_KEYS, 2 * NA_QT), lambda p, c: (p, c, 0, 0)),
        out_shape=jax.ShapeDtypeStruct((H // 2, NA_CLASSES, NA_KEYS, 2 * NA_QT), F32),
        compiler_params=_params(2),
        name="na_bias",
    )(toep)


def _out_kernel(zna_ref, zmla_ref, zgqa_ref, x_ref, p_ref, wna_ref, wmla_ref, wgqa_ref,
                pn_ref, wpg_ref, wpe_ref, fn_ref, o_ref, x1_ref, hn_ref, *, x_fm, final):
    tm = x1_ref.shape[1]
    chunks = [slice(r, r + OUT_ROW_CHUNK) for r in range(0, D_MODEL, OUT_ROW_CHUNK)]
    xt = x_ref[...] if x_fm else x_ref[...].T
    ss = jnp.zeros((1, tm), F32)
    for r in chunks:
        x1 = xt[r, :] + (_dot(wna_ref[r, :], zna_ref[...]) + _dot(wmla_ref[r, :], zmla_ref[...])
                         + _dot(wgqa_ref[r, :], zgqa_ref[...]))
        x1_ref[r, :] = x1
        ss = ss + jnp.sum(x1 * x1, axis=0, keepdims=True)
    inv = lax.rsqrt(ss * (1.0 / D_MODEL) + EPS)
    for r in chunks:
        hn_ref[r, :] = (x1_ref[r, :] * inv * pn_ref[r, :]).astype(BF16)
    pb = p_ref[...].astype(BF16)
    ss = jnp.zeros((1, tm), F32)
    for r in chunks:
        gate = 1.0 / (1.0 + jnp.exp(-_dot(wpg_ref[r, :], hn_ref[...])))
        x2 = x1_ref[r, :] + _dot_nt(wpe_ref[r, :], pb) * gate
        if final:
            x1_ref[r, :] = x2
            ss = ss + jnp.sum(x2 * x2, axis=0, keepdims=True)
        else:
            o_ref[r, :] = x2
    if final:
        inv = lax.rsqrt(ss * (1.0 / D_MODEL) + EPS)
        for r in chunks:
            o_ref[:, r] = (x1_ref[r, :] * inv * fn_ref[r, :]).T


def _out_ple(zna, zmla, zgqa, x, p, wo_na, wo_mla, wo_gqa, pn_col, wpg_t, wpe_t, fn_col, *, x_fm, final):
    T = p.shape[0]
    tm = TM_OUT
    x_spec = pl.BlockSpec((D_MODEL, tm), lambda i: (0, i)) if x_fm else pl.BlockSpec((tm, D_MODEL), lambda i: (i, 0))
    if final:
        out_spec, out_shape = pl.BlockSpec((tm, D_MODEL), lambda i: (i, 0)), (T, D_MODEL)
    else:
        out_spec, out_shape = pl.BlockSpec((D_MODEL, tm), lambda i: (0, i)), (D_MODEL, T)
    return pl.pallas_call(
        functools.partial(_out_kernel, x_fm=x_fm, final=final),
        grid=(T // tm,),
        in_specs=[
            pl.BlockSpec((NA_WIDTH, tm), lambda i: (0, i)),
            pl.BlockSpec((MLA_WIDTH, tm), lambda i: (0, i)),
            pl.BlockSpec((GQA_WIDTH, tm), lambda i: (0, i)),
            x_spec,
            pl.BlockSpec((tm, PLE_DIM), lambda i: (i, 0)),
            _const_spec((D_MODEL, NA_WIDTH)),
            _const_spec((D_MODEL, MLA_WIDTH)),
            _const_spec((D_MODEL, GQA_WIDTH)),
            _const_spec((D_MODEL, 1)),
            _const_spec((D_MODEL, D_MODEL)),
            _const_spec((D_MODEL, PLE_DIM)),
            _const_spec((D_MODEL, 1)),
        ],
        out_specs=out_spec,
        out_shape=jax.ShapeDtypeStruct(out_shape, F32),
        scratch_shapes=[pltpu.VMEM((D_MODEL, tm), F32), pltpu.VMEM((D_MODEL, tm), BF16)],
        compiler_params=_params(1),
        name="out_ple",
    )(zna, zmla, zgqa, x, p, wo_na, wo_mla, wo_gqa, pn_col, wpg_t, wpe_t, fn_col)


def _rope_tables_t(pos, dim):
    inv = ROPE_THETA ** (-jnp.arange(0, dim, 2, dtype=F32) / dim)
    ang = pos.astype(F32)[:, None] * inv[None, :]
    ang = jnp.concatenate([ang, ang], axis=-1)
    return jnp.cos(ang).T, jnp.sin(ang).T


def _tables(S):
    t = jnp.arange(S)
    cos_t, sin_t = _rope_tables_t(t, MLA_ROPE)
    cos_r, sin_r = _rope_tables_t(t // GRID_W, GQA_HEAD_DIM // 2)
    cos_c, sin_c = _rope_tables_t(t % GRID_W, GQA_HEAD_DIM // 2)
    return (jnp.concatenate([cos_t, sin_t], axis=0),
            jnp.concatenate([cos_r, cos_c, sin_r, sin_c], axis=0))


def _layer_params(i, norm_g, w_in, na_rpb, mla_q_a_norm, mla_w_q_b, mla_kv_a_norm, mla_w_kv_b,
                  gqa_q_norm, gqa_k_norm, w_out, ple_norm, w_pe, w_pg):
    wt = w_in[i].T
    o = 0
    w_na = wt[o:o + 4 * NA_WIDTH]
    o += 4 * NA_WIDTH
    n_mla = MLA_Q_LORA + MLA_KV_LORA + MLA_ROPE + MLA_WIDTH
    w_mla = wt[o:o + n_mla]
    o += n_mla
    w_gqa = wt[o:]
    wo_t = w_out[i].T
    return dict(
        norm_g=norm_g[i][None, :], norm_g_col=norm_g[i][:, None],
        w_na=w_na.astype(BF16), w_mla=w_mla.astype(BF16), w_gqa=w_gqa.astype(BF16),
        qan=mla_q_a_norm[i][:, None], wqb_t=mla_w_q_b[i].T.astype(BF16),
        kvan=mla_kv_a_norm[i][:, None], wkvb_t=mla_w_kv_b[i].T.astype(BF16),
        qn=gqa_q_norm[i][:, None], kn=gqa_k_norm[i][:, None],
        bias=_na_bias_table(na_rpb[i]),
        wo_na=wo_t[:, :NA_WIDTH].astype(BF16),
        wo_mla=wo_t[:, NA_WIDTH:NA_WIDTH + MLA_WIDTH].astype(BF16),
        wo_gqa=wo_t[:, NA_WIDTH + MLA_WIDTH:].astype(BF16),
        ple_norm=ple_norm[i][:, None], wpg_t=w_pg[i].T.astype(BF16), wpe_t=w_pe[i].T.astype(BF16),
    )


def _run_trunk(x, p, layers, final_norm):
    B, S, _ = x.shape
    T = B * S
    rows = S // GRID_W
    assert S % TM_PROJ == 0 and S % TM_OUT == 0 and S % GRID_W == 0
    assert rows >= NA_KEY_ROWS and rows % NA_ROWS_PER_STEP == 0 and rows // NA_ROWS_PER_STEP >= NA_CLASSES
    tab_mla, tab_gqa = _tables(S)
    xf = x.reshape(T, D_MODEL)
    fn = final_norm[:, None]
    depth = len(layers)
    for i, L in enumerate(layers):
        x_fm = i > 0
        g = L["norm_g_col"] if x_fm else L["norm_g"]
        pf = p[i].reshape(T, PLE_DIM)
        na_qt, na_k, na_vt, na_gt = _proj_na(xf, g, L["w_na"], x_fm=x_fm)
        ml_qt, ml_k, ml_vt, ml_gt = _proj_mla(xf, g, L["w_mla"], L["qan"], L["wqb_t"],
                                              L["kvan"], L["wkvb_t"], tab_mla, S, x_fm=x_fm)
        gq_qt, gq_k, gq_vt, gq_gt = _proj_gqa(xf, g, L["w_gqa"], L["qn"], L["kn"], tab_gqa, S, x_fm=x_fm)
        z_na = _na_attn(na_qt, na_k, na_vt, L["bias"], na_gt, B=B, S=S)
        z_mla = _flash_mla(ml_qt, ml_k, ml_vt, ml_gt, B=B, S=S)
        z_gqa = _flash_gqa(gq_qt, gq_k, gq_vt, gq_gt, B=B, S=S)
        xf = _out_ple(z_na, z_mla, z_gqa, xf, pf, L["wo_na"], L["wo_mla"], L["wo_gqa"], L["ple_norm"],
                      L["wpg_t"], L["wpe_t"], fn, x_fm=x_fm, final=(i == depth - 1))
    return xf.reshape(B, S, D_MODEL)


def kernel(x_prompt, x_sample, p_prompt, p_sample, norm_g, w_in, na_rpb, mla_q_a_norm, mla_w_q_b,
           mla_kv_a_norm, mla_w_kv_b, gqa_q_norm, gqa_k_norm, w_out, ple_norm, w_pe, w_pg, final_norm):
    depth = w_in.shape[0]
    layers = [_layer_params(i, norm_g, w_in, na_rpb, mla_q_a_norm, mla_w_q_b, mla_kv_a_norm, mla_w_kv_b,
                            gqa_q_norm, gqa_k_norm, w_out, ple_norm, w_pe, w_pg) for i in range(depth)]
    y_prompt = _run_trunk(x_prompt, p_prompt, layers, final_norm)
    y_sample = _run_trunk(x_sample, p_sample, layers, final_norm)
    return (y_prompt, y_sample)
```

```python
import functools
import math

import jax
import jax.numpy as jnp
from jax import lax
from jax.experimental import pallas as pl
from jax.experimental.pallas import tpu as pltpu

F32 = jnp.float32
BF16 = jnp.bfloat16

D_MODEL = 2048
GRID_W = 64
PLE_DIM = 256
EPS = 1e-6
ROPE_THETA = 10000.0

NA_HEADS, NA_HEAD_DIM, NA_WIN_H, NA_WIN_W = 8, 64, 8, 16
NA_WIDTH = NA_HEADS * NA_HEAD_DIM
MLA_HEADS, MLA_NOPE, MLA_ROPE, MLA_V = 6, 128, 64, 128
MLA_Q_LORA, MLA_KV_LORA = 512, 256
MLA_WIDTH = MLA_HEADS * MLA_V
MLA_QK = MLA_NOPE + MLA_ROPE
MLA_QK_PAD = 256
GQA_HEADS, GQA_KV_HEADS, GQA_HEAD_DIM = 12, 4, 64
GQA_GROUP = GQA_HEADS // GQA_KV_HEADS
GQA_WIDTH = GQA_HEADS * GQA_HEAD_DIM
GQA_KV_WIDTH = GQA_KV_HEADS * GQA_HEAD_DIM

VMEM_LIMIT_BYTES = 56 * 1024 * 1024

TM_PROJ = 1024
TM_OUT = 512
OUT_ROW_CHUNK = 256
SUM_ROWS = 16
FLASH_TQ_LONG = 1024
FLASH_TQ_SHORT = 2048
FLASH_KEY_TILES = {GQA_HEAD_DIM: (256, 16), MLA_V: (512, 8)}
NA_ROWS_PER_STEP = 4
NA_KEY_ROWS = 12
NA_KEYS = NA_KEY_ROWS * GRID_W
NA_QT = NA_ROWS_PER_STEP * GRID_W
NA_CLASSES = 3
NA_TQ = 4096
NEG_BIG = -1e30
LOG2E = 1.4426950408889634

_NT = (((1,), (1,)), ((), ()))


def _dot(a, b):
    return jnp.dot(a, b, preferred_element_type=F32)


def _dot_nt(a, b):
    return lax.dot_general(a, b, _NT, preferred_element_type=F32)


def _params(n_axes):
    return pltpu.CompilerParams(
        dimension_semantics=("arbitrary",) * n_axes,
        vmem_limit_bytes=VMEM_LIMIT_BYTES,
    )


def _const_spec(shape):
    nd = len(shape)
    return pl.BlockSpec(shape, lambda *_: (0,) * nd, pipeline_mode=pl.Buffered(1))


def _silu(x):
    return x * (1.0 / (1.0 + jnp.exp(-x)))


def _rms_rows(x, g_row):
    ms = jnp.mean(x * x, axis=-1, keepdims=True)
    return x * lax.rsqrt(ms + EPS) * g_row


def _rms_cols(xt, g_col):
    ms = jnp.mean(xt * xt, axis=0, keepdims=True)
    return xt * lax.rsqrt(ms + EPS) * g_col


def _normed_matmul(x_ref, g_ref, x_fm):
    if x_fm:
        ht = _rms_cols(x_ref[...], g_ref[...]).astype(BF16)
        return lambda w: _dot(w, ht)
    h = _rms_rows(x_ref[...], g_ref[...]).astype(BF16)
    return lambda w: _dot_nt(w, h)


def _x_specs(tm, x_fm):
    if x_fm:
        return [pl.BlockSpec((D_MODEL, tm), lambda i: (0, i)), _const_spec((D_MODEL, 1))]
    return [pl.BlockSpec((tm, D_MODEL), lambda i: (i, 0)), _const_spec((1, D_MODEL))]


def _rot_half_cols(xt, n):
    return jnp.concatenate([-xt[n:2 * n], xt[0:n]], axis=0)


def _proj_na_kernel(x_ref, g_ref, w_ref, qt_ref, k_ref, vt_ref, gt_ref, *, x_fm):
    mm = _normed_matmul(x_ref, g_ref, x_fm)
    W = NA_WIDTH
    qt_ref[...] = (mm(w_ref[0:W, :]) * (NA_HEAD_DIM ** -0.5 * LOG2E)).astype(BF16)
    kt = mm(w_ref[W:2 * W, :])
    for p in range(NA_HEADS // 2):
        k_ref[p] = kt[128 * p:128 * (p + 1), :].T.astype(BF16)
    vt_ref[...] = mm(w_ref[2 * W:3 * W, :]).astype(BF16)
    gt_ref[...] = _silu(mm(w_ref[3 * W:4 * W, :]))


def _proj_na(x, norm_g, wt, *, x_fm):
    T = x.shape[1 if x_fm else 0]
    tm = TM_PROJ
    return pl.pallas_call(
        functools.partial(_proj_na_kernel, x_fm=x_fm),
        grid=(T // tm,),
        in_specs=_x_specs(tm, x_fm) + [
            _const_spec((4 * NA_WIDTH, D_MODEL)),
        ],
        out_specs=[
            pl.BlockSpec((NA_WIDTH, tm), lambda i: (0, i)),
            pl.BlockSpec((NA_HEADS // 2, tm, 128), lambda i: (0, i, 0)),
            pl.BlockSpec((NA_WIDTH, tm), lambda i: (0, i)),
            pl.BlockSpec((NA_WIDTH, tm), lambda i: (0, i)),
        ],
        out_shape=[
            jax.ShapeDtypeStruct((NA_WIDTH, T), BF16),
            jax.ShapeDtypeStruct((NA_HEADS // 2, T, 128), BF16),
            jax.ShapeDtypeStruct((NA_WIDTH, T), BF16),
            jax.ShapeDtypeStruct((NA_WIDTH, T), F32),
        ],
        compiler_params=_params(1),
        name="proj_na",
    )(x, norm_g, wt)


def _proj_gqa_kernel(x_ref, g_ref, w_ref, qn_ref, kn_ref, tab_ref, qt_ref, k_ref, vt_ref, gt_ref, *, x_fm):
    mm = _normed_matmul(x_ref, g_ref, x_fm)
    cos = tab_ref[0:GQA_HEAD_DIM, :]
    sin = tab_ref[GQA_HEAD_DIM:2 * GQA_HEAD_DIM, :]
    q4 = GQA_HEAD_DIM // 4

    def norm_rope(xt, g_col):
        y = _rms_cols(xt, g_col)
        rot = jnp.concatenate([-y[q4:2 * q4], y[0:q4], -y[3 * q4:4 * q4], y[2 * q4:3 * q4]], axis=0)
        return y * cos + rot * sin

    o = 0
    qt = mm(w_ref[o:o + GQA_WIDTH, :])
    for hd in range(GQA_HEADS):
        r = slice(GQA_HEAD_DIM * hd, GQA_HEAD_DIM * (hd + 1))
        qt_ref[r, :] = (norm_rope(qt[r, :], qn_ref[...]) * (GQA_HEAD_DIM ** -0.5 * LOG2E)).astype(BF16)
    o += GQA_WIDTH
    kt = mm(w_ref[o:o + GQA_KV_WIDTH, :])
    for p in range(GQA_KV_HEADS // 2):
        pair = jnp.concatenate(
            [norm_rope(kt[GQA_HEAD_DIM * (2 * p + e):GQA_HEAD_DIM * (2 * p + e + 1), :], kn_ref[...])
             for e in range(2)], axis=0)
        k_ref[p] = pair.T.astype(BF16)
    o += GQA_KV_WIDTH
    vt_ref[...] = mm(w_ref[o:o + GQA_KV_WIDTH, :]).astype(BF16)
    o += GQA_KV_WIDTH
    gt_ref[...] = _silu(mm(w_ref[o:o + GQA_WIDTH, :]))


def _proj_gqa(x, norm_g, wt, qn_col, kn_col, tabs, S, *, x_fm):
    T = x.shape[1 if x_fm else 0]
    tm = TM_PROJ
    nb = S // tm
    rows = 2 * GQA_WIDTH + 2 * GQA_KV_WIDTH
    return pl.pallas_call(
        functools.partial(_proj_gqa_kernel, x_fm=x_fm),
        grid=(T // tm,),
        in_specs=_x_specs(tm, x_fm) + [
            _const_spec((rows, D_MODEL)),
            _const_spec((GQA_HEAD_DIM, 1)),
            _const_spec((GQA_HEAD_DIM, 1)),
            pl.BlockSpec((2 * GQA_HEAD_DIM, tm), lambda i: (0, i % nb)),
        ],
        out_specs=[
            pl.BlockSpec((GQA_WIDTH, tm), lambda i: (0, i)),
            pl.BlockSpec((GQA_KV_HEADS // 2, tm, 128), lambda i: (0, i, 0)),
            pl.BlockSpec((GQA_KV_WIDTH, tm), lambda i: (0, i)),
            pl.BlockSpec((GQA_WIDTH, tm), lambda i: (0, i)),
        ],
        out_shape=[
            jax.ShapeDtypeStruct((GQA_WIDTH, T), BF16),
            jax.ShapeDtypeStruct((GQA_KV_HEADS // 2, T, 128), BF16),
            jax.ShapeDtypeStruct((GQA_KV_WIDTH, T), BF16),
            jax.ShapeDtypeStruct((GQA_WIDTH, T), F32),
        ],
        compiler_params=_params(1),
        name="proj_gqa",
    )(x, norm_g, wt, qn_col, kn_col, tabs)


def _proj_mla_kernel(x_ref, g_ref, w_ref, qan_ref, wqb_ref, kvan_ref, wkvb_ref, tab_ref,
                     qt_ref, k_ref, vt_ref, gt_ref, *, x_fm):
    mm = _normed_matmul(x_ref, g_ref, x_fm)
    cos = tab_ref[0:MLA_ROPE, :]
    sin = tab_ref[MLA_ROPE:2 * MLA_ROPE, :]
    tm = tab_ref.shape[1]
    half = MLA_ROPE // 2
    scale = MLA_QK ** -0.5 * LOG2E

    def rope(xt):
        return xt * cos + _rot_half_cols(xt, half) * sin

    o = 0
    cq = _rms_cols(mm(w_ref[o:o + MLA_Q_LORA, :]), qan_ref[...]).astype(BF16)
    qt = _dot(wqb_ref[...], cq)
    zpad = jnp.zeros((MLA_QK_PAD - MLA_QK, tm), BF16)
    for hd in range(MLA_HEADS):
        b = MLA_QK * hd
        ob = MLA_QK_PAD * hd
        qt_ref[ob:ob + MLA_NOPE, :] = (qt[b:b + MLA_NOPE, :] * scale).astype(BF16)
        qt_ref[ob + MLA_NOPE:ob + MLA_QK, :] = (rope(qt[b + MLA_NOPE:b + MLA_QK, :]) * scale).astype(BF16)
        qt_ref[ob + MLA_QK:ob + MLA_QK_PAD, :] = zpad
    o += MLA_Q_LORA
    ckv = _rms_cols(mm(w_ref[o:o + MLA_KV_LORA, :]), kvan_ref[...]).astype(BF16)
    kvt = _dot(wkvb_ref[...], ckv)
    o += MLA_KV_LORA
    krt = rope(mm(w_ref[o:o + MLA_ROPE, :]))
    kr = jnp.concatenate([krt, jnp.zeros_like(krt)], axis=0).T.astype(BF16)
    o += MLA_ROPE
    for hd in range(MLA_HEADS):
        b = (MLA_NOPE + MLA_V) * hd
        k_ref[hd, :, 0:MLA_NOPE] = kvt[b:b + MLA_NOPE, :].T.astype(BF16)
        k_ref[hd, :, MLA_NOPE:MLA_QK_PAD] = kr
        vt_ref[MLA_V * hd:MLA_V * (hd + 1), :] = kvt[b + MLA_NOPE:b + MLA_NOPE + MLA_V, :].astype(BF16)
    gt_ref[...] = _silu(mm(w_ref[o:o + MLA_WIDTH, :]))


def _proj_mla(x, norm_g, wt, qan_col, wqb_t, kvan_col, wkvb_t, tabs, S, *, x_fm):
    T = x.shape[1 if x_fm else 0]
    tm = TM_PROJ
    nb = S // tm
    rows = MLA_Q_LORA + MLA_KV_LORA + MLA_ROPE + MLA_WIDTH
    return pl.pallas_call(
        functools.partial(_proj_mla_kernel, x_fm=x_fm),
        grid=(T // tm,),
        in_specs=_x_specs(tm, x_fm) + [
            _const_spec((rows, D_MODEL)),
            _const_spec((MLA_Q_LORA, 1)),
            _const_spec((MLA_HEADS * MLA_QK, MLA_Q_LORA)),
            _const_spec((MLA_KV_LORA, 1)),
            _const_spec((MLA_HEADS * (MLA_NOPE + MLA_V), MLA_KV_LORA)),
            pl.BlockSpec((2 * MLA_ROPE, tm), lambda i: (0, i % nb)),
        ],
        out_specs=[
            pl.BlockSpec((MLA_HEADS * MLA_QK_PAD, tm), lambda i: (0, i)),
            pl.BlockSpec((MLA_HEADS, tm, MLA_QK_PAD), lambda i: (0, i, 0)),
            pl.BlockSpec((MLA_WIDTH, tm), lambda i: (0, i)),
            pl.BlockSpec((MLA_WIDTH, tm), lambda i: (0, i)),
        ],
        out_shape=[
            jax.ShapeDtypeStruct((MLA_HEADS * MLA_QK_PAD, T), BF16),
            jax.ShapeDtypeStruct((MLA_HEADS, T, MLA_QK_PAD), BF16),
            jax.ShapeDtypeStruct((MLA_WIDTH, T), BF16),
            jax.ShapeDtypeStruct((MLA_WIDTH, T), F32),
        ],
        compiler_params=_params(1),
        name="proj_mla",
    )(x, norm_g, wt, qan_col, wqb_t, kvan_col, wkvb_t, tabs)


def _flash_kernel(qt_ref, k_ref, vt_ref, gt_ref, o_ref, s_ref, acc_ref, *, n_chunks, tk, unroll, q_half_of):
    g = pl.program_id(1)
    qt = qt_ref[...]
    if q_half_of is not None:
        half = q_half_of(g)
        z = jnp.zeros_like(qt)
        qt = jnp.concatenate([jnp.where(half == 0, qt, z), jnp.where(half == 1, qt, z)], axis=0)
    tq = qt.shape[1]
    dv = vt_ref.shape[0]
    acc_ref[...] = jnp.zeros_like(acc_ref)
    ones = jnp.ones((SUM_ROWS, tk), BF16)

    def stage1(c, slot):
        start = pl.multiple_of(c * tk, tk)
        s = _dot(k_ref[0, pl.ds(start, tk), :], qt)
        s_ref[slot] = s
        return jnp.max(s, axis=0, keepdims=True)

    def stage2(c, slot, m, m_cur):
        start = pl.multiple_of(c * tk, tk)
        m_new = jnp.maximum(m, m_cur)
        alpha = jnp.exp2(m - m_new)
        p = jnp.exp2(s_ref[slot] - m_new).astype(BF16)
        v1 = jnp.concatenate([vt_ref[:, pl.ds(start, tk)], ones], axis=0)
        acc_ref[...] = alpha * acc_ref[...] + _dot(v1, p)
        return m_new

    def group(i, carry):
        m, m_cur = carry
        c = unroll * i
        for u in range(unroll):
            m_nxt = stage1(c + u + 1, (u + 1) % 2)
            m = stage2(c + u, u % 2, m, m_cur)
            m_cur = m_nxt
        return m, m_cur

    n_groups = (n_chunks - 1) // unroll
    m = jnp.full((1, tq), -jnp.inf, F32)
    m_cur = stage1(0, 0)
    m, m_cur = lax.fori_loop(0, n_groups, group, (m, m_cur))
    for c in range(n_groups * unroll, n_chunks):
        m_nxt = stage1(c + 1, (c + 1) % 2) if c + 1 < n_chunks else None
        m = stage2(c, c % 2, m, m_cur)
        m_cur = m_nxt
    l = acc_ref[dv:dv + 1, :]
    o_ref[...] = (acc_ref[0:dv, :] * (1.0 / l) * gt_ref[...]).astype(BF16)


def _flash(qt, k, vt, gt, *, B, S, heads, dk, dv, kv_of, q_half_of=None, name):
    tq = S if S <= FLASH_TQ_SHORT else FLASH_TQ_LONG
    tk, unroll = FLASH_KEY_TILES[dv]
    nq = S // tq
    dkk = k.shape[2]
    n_chunks = S // tk
    assert S % tq == 0 and S % tk == 0 and unroll % 2 == 0
    kern = functools.partial(_flash_kernel, n_chunks=n_chunks, tk=tk, unroll=unroll, q_half_of=q_half_of)
    k_of, v_of = kv_of
    return pl.pallas_call(
        kern,
        grid=(B, heads, nq),
        in_specs=[
            pl.BlockSpec((dk, tq), lambda b, g, i: (g, b * nq + i)),
            pl.BlockSpec((1, S, dkk), lambda b, g, i: (k_of(g), b, 0)),
            pl.BlockSpec((dv, S), lambda b, g, i: (v_of(g), b)),
            pl.BlockSpec((dv, tq), lambda b, g, i: (g, b * nq + i)),
        ],
        out_specs=pl.BlockSpec((dv, tq), lambda b, g, i: (g, b * nq + i)),
        out_shape=jax.ShapeDtypeStruct((heads * dv, B * S), BF16),
        scratch_shapes=[pltpu.VMEM((2, tk, tq), F32), pltpu.VMEM((dv + SUM_ROWS, tq), F32)],
        compiler_params=_params(3),
        name=name,
    )(qt, k, vt, gt)


def _flash_mla(qt, k, vt, gt, *, B, S):
    return _flash(qt, k, vt, gt, B=B, S=S, heads=MLA_HEADS, dk=MLA_QK_PAD, dv=MLA_V,
                  kv_of=(lambda g: g, lambda g: g), name="flash_mla")


def _flash_gqa(qt, k, vt, gt, *, B, S):
    return _flash(qt, k, vt, gt, B=B, S=S, heads=GQA_HEADS, dk=GQA_HEAD_DIM, dv=GQA_HEAD_DIM,
                  kv_of=(lambda g: g // (2 * GQA_GROUP), lambda g: g // GQA_GROUP),
                  q_half_of=lambda g: (g // GQA_GROUP) % 2, name="flash_gqa")


def _na_kernel(qt_ref, k_ref, vt_ref, bias_ref, gt_ref, o_ref, *, rows, n_inner):
    i = pl.program_id(2)
    half_rows = rows // NA_ROWS_PER_STEP
    D = NA_HEAD_DIM
    top = lax.broadcasted_iota(jnp.int32, (2 * D, NA_QT), 0) < D
    ones = jnp.ones((SUM_ROWS, NA_KEYS), BF16)

    def scores(t):
        j = i * n_inner + t
        q = qt_ref[:, t * NA_QT:(t + 1) * NA_QT]
        z = jnp.zeros_like(q)
        q2 = jnp.concatenate([jnp.where(top, q, z), jnp.where(top, z, q)], axis=1)
        kb = jnp.clip(NA_ROWS_PER_STEP * j - NA_WIN_H // 2, 0, rows - NA_KEY_ROWS)
        ks = pl.multiple_of(kb * GRID_W, 2 * GRID_W)
        cls = jnp.where(j == 0, 0, jnp.where(j == half_rows - 1, 2, 1))
        return _dot(k_ref[0, pl.ds(ks, NA_KEYS), :], q2) + bias_ref[0, cls], ks

    def finish(t, s, ks):
        m = jnp.max(s, axis=0, keepdims=True)
        p = jnp.exp2(s - m).astype(BF16)
        for e in range(2):
            v1 = jnp.concatenate([vt_ref[D * e:D * (e + 1), pl.ds(ks, NA_KEYS)], ones], axis=0)
            y = _dot(v1, p[:, e * NA_QT:(e + 1) * NA_QT])
            g = gt_ref[D * e:D * (e + 1), t * NA_QT:(t + 1) * NA_QT]
            o_ref[D * e:D * (e + 1), t * NA_QT:(t + 1) * NA_QT] = (y[0:D] * (1.0 / y[D:D + 1]) * g).astype(BF16)

    cur = scores(0)
    for t in range(n_inner):
        nxt = scores(t + 1) if t + 1 < n_inner else None
        finish(t, *cur)
        cur = nxt


def _na_attn(qt, k, vt, bias, gt, *, B, S):
    rows = S // GRID_W
    tq = math.gcd(NA_TQ, S)
    assert tq % NA_QT == 0
    nq = S // tq
    n_inner = tq // NA_QT
    kern = functools.partial(_na_kernel, rows=rows, n_inner=n_inner)
    D2 = 2 * NA_HEAD_DIM
    return pl.pallas_call(
        kern,
        grid=(B, NA_HEADS // 2, nq),
        in_specs=[
            pl.BlockSpec((D2, tq), lambda b, h, i: (h, b * nq + i)),
            pl.BlockSpec((1, S, D2), lambda b, h, i: (h, b, 0)),
            pl.BlockSpec((D2, S), lambda b, h, i: (h, b)),
            pl.BlockSpec((1, NA_CLASSES, NA_KEYS, 2 * NA_QT), lambda b, h, i: (h, 0, 0, 0)),
            pl.BlockSpec((D2, tq), lambda b, h, i: (h, b * nq + i)),
        ],
        out_specs=pl.BlockSpec((D2, tq), lambda b, h, i: (h, b * nq + i)),
        out_shape=jax.ShapeDtypeStruct((NA_WIDTH, B * S), BF16),
        compiler_params=_params(3),
        name="na_attn",
    )(qt, k, vt, bias, gt)


def _na_bias_kernel(toep_ref, o_ref):
    c = pl.program_id(1)
    off = jnp.where(c == 0, NA_WIN_H - 1, jnp.where(c == 1, NA_WIN_H - 1 - NA_ROWS_PER_STEP, -1))
    kc = lax.broadcasted_iota(jnp.int32, (GRID_W, GRID_W), 0)
    qc = lax.broadcasted_iota(jnp.int32, (GRID_W, GRID_W), 1)
    cs = jnp.clip(qc - NA_WIN_W // 2, 0, GRID_W - NA_WIN_W)
    col_ok = (kc >= cs) & (kc < cs + NA_WIN_W)
    for i in range(NA_KEY_ROWS):
        for e in range(2):
            blocks = []
            for rr in range(NA_ROWS_PER_STEP):
                lo = jnp.where(c == 0, 0, jnp.where(c == 1, rr, NA_KEY_ROWS - NA_WIN_H))
                row_ok = (i >= lo) & (i < lo + NA_WIN_H)
                d = jnp.clip(i - rr + off, 0, 2 * NA_WIN_H - 2)
                blocks.append(jnp.where(col_ok & row_ok, toep_ref[e, d], NEG_BIG))
            o_ref[0, 0, i * GRID_W:(i + 1) * GRID_W, e * NA_QT:(e + 1) * NA_QT] = jnp.concatenate(blocks, axis=1)


def _na_bias_table(rpb):
    assert (NA_ROWS_PER_STEP, NA_KEY_ROWS, NA_WIN_H) == (4, 12, 8)
    H, nd, nw = rpb.shape
    lo = (GRID_W - 1) - (NA_WIN_W - 1)
    r128 = jnp.pad(rpb * LOG2E, ((0, 0), (0, 0), (lo, 128 - nw - lo)))
    y = jnp.tile(r128, (1, 1, GRID_W))[:, :, :GRID_W * 127].reshape(H, nd, GRID_W, 127)
    toep = jnp.swapaxes(y[:, :, :, GRID_W - 1:2 * GRID_W - 1], 2, 3)
    return pl.pallas_call(
        _na_bias_kernel,
        grid=(H // 2, NA_CLASSES),
        in_specs=[pl.BlockSpec((2, nd, GRID_W, GRID_W), lambda p, c: (p, 0, 0, 0))],
        out_specs=pl.BlockSpec((1, 1, NA_KEYS, 2 * NA_QT), lambda p, c: (p, c, 0, 0)),
        out_shape=jax.ShapeDtypeStruct((H // 2, NA_CLASSES, NA_KEYS, 2 * NA_QT), F32),
        compiler_params=_params(2),
        name="na_bias",
    )(toep)


def _out_kernel(zna_ref, zmla_ref, zgqa_ref, x_ref, p_ref, wna_ref, wmla_ref, wgqa_ref,
                pn_ref, wpg_ref, wpe_ref, fn_ref, o_ref, x1_ref, hn_ref, *, x_fm, final):
    tm = x1_ref.shape[1]
    chunks = [slice(r, r + OUT_ROW_CHUNK) for r in range(0, D_MODEL, OUT_ROW_CHUNK)]
    xt = x_ref[...] if x_fm else x_ref[...].T
    ss = jnp.zeros((1, tm), F32)
    for r in chunks:
        x1 = xt[r, :] + (_dot(wna_ref[r, :], zna_ref[...]) + _dot(wmla_ref[r, :], zmla_ref[...])
                         + _dot(wgqa_ref[r, :], zgqa_ref[...]))
        x1_ref[r, :] = x1
        ss = ss + jnp.sum(x1 * x1, axis=0, keepdims=True)
    inv = lax.rsqrt(ss * (1.0 / D_MODEL) + EPS)
    for r in chunks:
        hn_ref[r, :] = (x1_ref[r, :] * inv * pn_ref[r, :]).astype(BF16)
    pb = p_ref[...].astype(BF16)
    ss = jnp.zeros((1, tm), F32)
    for r in chunks:
        gate = 1.0 / (1.0 + jnp.exp(-_dot(wpg_ref[r, :], hn_ref[...])))
        x2 = x1_ref[r, :] + _dot_nt(wpe_ref[r, :], pb) * gate
        if final:
            x1_ref[r, :] = x2
            ss = ss + jnp.sum(x2 * x2, axis=0, keepdims=True)
        else:
            o_ref[r, :] = x2
    if final:
        inv = lax.rsqrt(ss * (1.0 / D_MODEL) + EPS)
        for r in chunks:
            o_ref[:, r] = (x1_ref[r, :] * inv * fn_ref[r, :]).T


def _out_ple(zna, zmla, zgqa, x, p, wo_na, wo_mla, wo_gqa, pn_col, wpg_t, wpe_t, fn_col, *, x_fm, final):
    T = p.shape[0]
    tm = TM_OUT
    x_spec = pl.BlockSpec((D_MODEL, tm), lambda i: (0, i)) if x_fm else pl.BlockSpec((tm, D_MODEL), lambda i: (i, 0))
    if final:
        out_spec, out_shape = pl.BlockSpec((tm, D_MODEL), lambda i: (i, 0)), (T, D_MODEL)
    else:
        out_spec, out_shape = pl.BlockSpec((D_MODEL, tm), lambda i: (0, i)), (D_MODEL, T)
    return pl.pallas_call(
        functools.partial(_out_kernel, x_fm=x_fm, final=final),
        grid=(T // tm,),
        in_specs=[
            pl.BlockSpec((NA_WIDTH, tm), lambda i: (0, i)),
            pl.BlockSpec((MLA_WIDTH, tm), lambda i: (0, i)),
            pl.BlockSpec((GQA_WIDTH, tm), lambda i: (0, i)),
            x_spec,
            pl.BlockSpec((tm, PLE_DIM), lambda i: (i, 0)),
            _const_spec((D_MODEL, NA_WIDTH)),
            _const_spec((D_MODEL, MLA_WIDTH)),
            _const_spec((D_MODEL, GQA_WIDTH)),
            _const_spec((D_MODEL, 1)),
            _const_spec((D_MODEL, D_MODEL)),
            _const_spec((D_MODEL, PLE_DIM)),
            _const_spec((D_MODEL, 1)),
        ],
        out_specs=out_spec,
        out_shape=jax.ShapeDtypeStruct(out_shape, F32),
        scratch_shapes=[pltpu.VMEM((D_MODEL, tm), F32), pltpu.VMEM((D_MODEL, tm), BF16)],
        compiler_params=_params(1),
        name="out_ple",
    )(zna, zmla, zgqa, x, p, wo_na, wo_mla, wo_gqa, pn_col, wpg_t, wpe_t, fn_col)


def _rope_tables_t(pos, dim):
    inv = ROPE_THETA ** (-jnp.arange(0, dim, 2, dtype=F32) / dim)
    ang = pos.astype(F32)[:, None] * inv[None, :]
    ang = jnp.concatenate([ang, ang], axis=-1)
    return jnp.cos(ang).T, jnp.sin(ang).T


def _tables(S):
    t = jnp.arange(S)
    cos_t, sin_t = _rope_tables_t(t, MLA_ROPE)
    cos_r, sin_r = _rope_tables_t(t // GRID_W, GQA_HEAD_DIM // 2)
    cos_c, sin_c = _rope_tables_t(t % GRID_W, GQA_HEAD_DIM // 2)
    return (jnp.concatenate([cos_t, sin_t], axis=0),
            jnp.concatenate([cos_r, cos_c, sin_r, sin_c], axis=0))


def _layer_params(i, norm_g, w_in, na_rpb, mla_q_a_norm, mla_w_q_b, mla_kv_a_norm, mla_w_kv_b,
                  gqa_q_norm, gqa_k_norm, w_out, ple_norm, w_pe, w_pg):
    wt = w_in[i].T
    o = 0
    w_na = wt[o:o + 4 * NA_WIDTH]
    o += 4 * NA_WIDTH
    n_mla = MLA_Q_LORA + MLA_KV_LORA + MLA_ROPE + MLA_WIDTH
    w_mla = wt[o:o + n_mla]
    o += n_mla
    w_gqa = wt[o:]
    wo_t = w_out[i].T
    return dict(
        norm_g=norm_g[i][None, :], norm_g_col=norm_g[i][:, None],
        w_na=w_na.astype(BF16), w_mla=w_mla.astype(BF16), w_gqa=w_gqa.astype(BF16),
        qan=mla_q_a_norm[i][:, None], wqb_t=mla_w_q_b[i].T.astype(BF16),
        kvan=mla_kv_a_norm[i][:, None], wkvb_t=mla_w_kv_b[i].T.astype(BF16),
        qn=gqa_q_norm[i][:, None], kn=gqa_k_norm[i][:, None],
        bias=_na_bias_table(na_rpb[i]),
        wo_na=wo_t[:, :NA_WIDTH].astype(BF16),
        wo_mla=wo_t[:, NA_WIDTH:NA_WIDTH + MLA_WIDTH].astype(BF16),
        wo_gqa=wo_t[:, NA_WIDTH + MLA_WIDTH:].astype(BF16),
        ple_norm=ple_norm[i][:, None], wpg_t=w_pg[i].T.astype(BF16), wpe_t=w_pe[i].T.astype(BF16),
    )


def _run_trunk(x, p, layers, final_norm):
    B, S, _ = x.shape
    T = B * S
    rows = S // GRID_W
    assert S % TM_PROJ == 0 and S % TM_OUT == 0 and S % GRID_W == 0
    assert rows >= NA_KEY_ROWS and rows % NA_ROWS_PER_STEP == 0 and rows // NA_ROWS_PER_STEP >= NA_CLASSES
    tab_mla, tab_gqa = _tables(S)
    xf = x.reshape(T, D_MODEL)
    fn = final_norm[:, None]
    depth = len(layers)
    for i, L in enumerate(layers):
        x_fm = i > 0
        g = L["norm_g_col"] if x_fm else L["norm_g"]
        pf = p[i].reshape(T, PLE_DIM)
        na_qt, na_k, na_vt, na_gt = _proj_na(xf, g, L["w_na"], x_fm=x_fm)
        ml_qt, ml_k, ml_vt, ml_gt = _proj_mla(xf, g, L["w_mla"], L["qan"], L["wqb_t"],
                                              L["kvan"], L["wkvb_t"], tab_mla, S, x_fm=x_fm)
        gq_qt, gq_k, gq_vt, gq_gt = _proj_gqa(xf, g, L["w_gqa"], L["qn"], L["kn"], tab_gqa, S, x_fm=x_fm)
        z_na = _na_attn(na_qt, na_k, na_vt, L["bias"], na_gt, B=B, S=S)
        z_mla = _flash_mla(ml_qt, ml_k, ml_vt, ml_gt, B=B, S=S)
        z_gqa = _flash_gqa(gq_qt, gq_k, gq_vt, gq_gt, B=B, S=S)
        xf = _out_ple(z_na, z_mla, z_gqa, xf, pf, L["wo_na"], L["wo_mla"], L["wo_gqa"], L["ple_norm"],
                      L["wpg_t"], L["wpe_t"], fn, x_fm=x_fm, final=(i == depth - 1))
    return xf.reshape(B, S, D_MODEL)


def kernel(x_prompt, x_sample, p_prompt, p_sample, norm_g, w_in, na_rpb, mla_q_a_norm, mla_w_q_b,
           mla_kv_a_norm, mla_w_kv_b, gqa_q_norm, gqa_k_norm, w_out, ple_norm, w_pe, w_pg, final_norm):
    depth = w_in.shape[0]
    layers = [_layer_params(i, norm_g, w_in, na_rpb, mla_q_a_norm, mla_w_q_b, mla_kv_a_norm, mla_w_kv_b,
                            gqa_q_norm, gqa_k_norm, w_out, ple_norm, w_pe, w_pg) for i in range(depth)]
    y_prompt = _run_trunk(x_prompt, p_prompt, layers, final_norm)
    y_sample = _run_trunk(x_sample, p_sample, layers, final_norm)
    return (y_prompt, y_sample)
```
